```python
import math
import jax
import jax.numpy as jnp
from jax import lax
import numpy as np

D_MODEL = 2048
BATCH = 8
SEQ = 2048
DEPTH = 4

GRID_W = 64
CTX_LEN = 256
HEAD_DIM = 128
NA_HEADS = D_MODEL // (2 * HEAD_DIM)
NA_WIN_ROWS = 8
NA_WIN_COLS = 16
GQA_Q_HEADS = D_MODEL // (2 * HEAD_DIM)
GQA_KV_HEADS = max(1, GQA_Q_HEADS // 4)
Q_BLOCK = 128
ROPE_BASE = 10000.0
NA_WIDTH = NA_HEADS * HEAD_DIM
GQA_Q_WIDTH = GQA_Q_HEADS * HEAD_DIM
GQA_KV_WIDTH = GQA_KV_HEADS * HEAD_DIM
Q_COLS = NA_WIDTH + GQA_Q_WIDTH
IN_COLS = Q_COLS + 2 * NA_WIDTH + 2 * GQA_KV_WIDTH
HY_ORDER = 2
HY_SHORT = 3
HY_EMB_DIM = 33
HY_FILTER_HIDDEN = 64
HY_TARGET = 1e-2
HY_FAST_DECAY = 0.3
HY_SLOW_DECAY = 1.5
HY_MOD_SHIFT = 0.05
D_FF = 256 * ((8 * D_MODEL // 3 + 255) // 256)
N_EXPERTS = 8
TOP_K = 2
MOE_BLOCK = 256
NORM_EPS = 1e-6
NEG_INF = -1e30
N_EVEN = (DEPTH + 1) // 2
N_ODD = DEPTH // 2

kernel_name = 'hybrid_natten_gqa_hyena_moe_diffusion_trunk'


def rms_norm(x, g):
    xf = x.astype(jnp.float32)
    y = xf * lax.rsqrt(jnp.mean(xf * xf, axis=-1, keepdims=True) + NORM_EPS)
    return (y * g.astype(jnp.float32)).astype(x.dtype)


def modulate(x, g, shift, scale):
    return rms_norm(x, g) * (1 + scale) + shift


def swiglu(h, w1, w3, w2):
    return (jax.nn.silu(h @ w1) * (h @ w3)) @ w2


def split_heads(t, n_heads):
    return t.reshape(t.shape[0], t.shape[1], n_heads, HEAD_DIM)


def axial_rope_tables(n_tok):
    t = jnp.arange(n_tok)
    row = (t // GRID_W).astype(jnp.float32)
    col = (t % GRID_W).astype(jnp.float32)
    n_freq = HEAD_DIM // 4
    inv_freq = ROPE_BASE ** (-jnp.arange(n_freq, dtype=jnp.float32) / n_freq)
    ang = jnp.concatenate([row[:, None] * inv_freq, col[:, None] * inv_freq], axis=-1)
    return jnp.cos(ang), jnp.sin(ang)


def apply_rope(x, cos, sin):
    xf = x.astype(jnp.float32)
    x1, x2 = jnp.split(xf, 2, axis=-1)
    cs, sn = cos[None, :, None, :], sin[None, :, None, :]
    return jnp.concatenate([x1 * cs - x2 * sn, x1 * sn + x2 * cs], axis=-1).astype(x.dtype)


def gqa_attend(q, k, v):
    b, nq, hq, hd = q.shape
    hkv = k.shape[2]
    qg = q.reshape(b, nq, hkv, hq // hkv, hd)
    s = jnp.einsum('bqkgd,btkd->bkgqt', qg, k).astype(jnp.float32) * (hd ** -0.5)
    p = jax.nn.softmax(s, axis=-1).astype(v.dtype)
    o = jnp.einsum('bkgqt,btkd->bqkgd', p, v)
    return o.reshape(b, nq, hq * hd)


def blocked_global_attention(q, k, v, kc, vc):
    b, s, hq, hd = q.shape
    keys = jnp.concatenate([kc, k], axis=1)
    vals = jnp.concatenate([vc, v], axis=1)
    nb = s // Q_BLOCK
    qb = q.reshape(b, nb, Q_BLOCK, hq, hd).swapaxes(0, 1)
    o = lax.map(lambda qi: gqa_attend(qi, keys, vals), qb)
    return o.swapaxes(0, 1).reshape(b, s, hq * hd)


def neighbourhood_attention(q, k, v, kc, vc, rpb):
    b, s, h, hd = q.shape
    rows = s // GRID_W
    kr = min(NA_WIN_ROWS, rows)
    scale = hd ** -0.5
    kg = k.reshape(b, rows, GRID_W, h, hd)
    vg = v.reshape(b, rows, GRID_W, h, hd)
    col = jnp.arange(GRID_W)
    col_start = jnp.clip(col - NA_WIN_COLS // 2, 0, GRID_W - NA_WIN_COLS)
    col_ok = (col[None, :] >= col_start[:, None]) & (col[None, :] < col_start[:, None] + NA_WIN_COLS)
    dc_idx = jnp.clip(col[None, :] - col[:, None] + NA_WIN_COLS - 1, 0, 2 * NA_WIN_COLS - 2)
    rpb_q = rpb.astype(jnp.float32)[:, :, dc_idx].transpose(0, 2, 1, 3)
    n_lat = kr * GRID_W

    def row_block(args):
        r, qr = args
        rs = jnp.clip(r - kr // 2, 0, rows - kr)
        ks = lax.dynamic_slice_in_dim(kg, rs, kr, axis=1)
        vs = lax.dynamic_slice_in_dim(vg, rs, kr, axis=1)
        dr_idx = rs + jnp.arange(kr) - r + NA_WIN_ROWS - 1
        bias = jnp.where(col_ok[None, :, None, :], rpb_q[:, :, dr_idx], NEG_INF)
        s_lat = jnp.einsum('bqhd,brkhd->bhqrk', qr, ks).astype(jnp.float32) * scale + bias[None]
        s_ctx = jnp.einsum('bqhd,bthd->bhqt', qr, kc).astype(jnp.float32) * scale
        scores = jnp.concatenate([s_lat.reshape(b, h, GRID_W, n_lat), s_ctx], axis=-1)
        p = jax.nn.softmax(scores, axis=-1).astype(v.dtype)
        p_lat = p[..., :n_lat].reshape(b, h, GRID_W, kr, GRID_W)
        return (jnp.einsum('bhqrk,brkhd->bqhd', p_lat, vs)
                + jnp.einsum('bhqt,bthd->bqhd', p[..., n_lat:], vc))

    q_rows = q.reshape(b, rows, GRID_W, h, hd).swapaxes(0, 1)
    o = lax.map(row_block, (jnp.arange(rows), q_rows))
    return o.swapaxes(0, 1).reshape(b, s, h * hd)


def hybrid_attention(h, hc, w_in, w_out, q_norm_g, k_norm_g, rpb, ctx_out):
    cuts = [NA_WIDTH, Q_COLS, Q_COLS + NA_WIDTH, Q_COLS + 2 * NA_WIDTH, Q_COLS + 2 * NA_WIDTH + GQA_KV_WIDTH]
    qa, qb, ka, va, kb, vb = jnp.split(h @ w_in, cuts, axis=-1)
    kac, vac, kbc, vbc = jnp.split(hc @ w_in[:, Q_COLS:], [c - Q_COLS for c in cuts[2:]], axis=-1)
    cos, sin = axial_rope_tables(h.shape[1])
    qa, ka, va = split_heads(qa, NA_HEADS), split_heads(ka, NA_HEADS), split_heads(va, NA_HEADS)
    kac, vac = split_heads(kac, NA_HEADS), split_heads(vac, NA_HEADS)
    qb = apply_rope(rms_norm(split_heads(qb, GQA_Q_HEADS), q_norm_g), cos, sin)
    kb = apply_rope(rms_norm(split_heads(kb, GQA_KV_HEADS), k_norm_g), cos, sin)
    vb = split_heads(vb, GQA_KV_HEADS)
    kbc = rms_norm(split_heads(kbc, GQA_KV_HEADS), k_norm_g)
    vbc = split_heads(vbc, GQA_KV_HEADS)
    o_a = neighbourhood_attention(qa, ka, va, kac, vac, rpb)
    o_b = blocked_global_attention(qb, kb, vb, kbc, vbc)
    y = jnp.concatenate([o_a, o_b], axis=-1) @ w_out
    if not ctx_out:
        return y, None
    qac, qbc = jnp.split(hc @ w_in[:, :Q_COLS], [NA_WIDTH], axis=-1)
    oc_a = gqa_attend(split_heads(qac, NA_HEADS), kac, vac)
    oc_b = gqa_attend(rms_norm(split_heads(qbc, GQA_Q_HEADS), q_norm_g), kbc, vbc)
    yc = jnp.concatenate([oc_a, oc_b], axis=-1) @ w_out
    return y, yc


def short_conv(u, w, b):
    n_tok = u.shape[1]
    pad = HY_SHORT // 2
    up = jnp.pad(u, ((0, 0), (pad, HY_SHORT - 1 - pad), (0, 0)))
    return sum(up[:, j:j + n_tok] * w[j] for j in range(HY_SHORT)) + b


def hyena_filters(n_tok, f_w1, f_b1, f_w2, f_b2, f_w3, freq):
    f32 = jnp.float32
    d = f_w3.shape[1] // (2 * HY_ORDER)
    bands = (HY_EMB_DIM - 1) // 2
    t = jnp.linspace(0.0, 1.0, n_tok, dtype=f32)[:, None]
    w = 2.0 * math.pi * jnp.arange(n_tok, dtype=f32)[:, None] / n_tok
    f = jnp.linspace(1e-4, bands - 1, bands, dtype=f32)[None, :]
    z = jnp.concatenate([t, jnp.cos(f * w), -jnp.sin(f * w)], axis=-1)
    fr = freq.astype(f32)
    a = jnp.sin(fr * (z @ f_w1.astype(f32) + f_b1.astype(f32)))
    a = jnp.sin(fr * (a @ f_w2.astype(f32) + f_b2.astype(f32)))
    filt = (a @ f_w3.astype(f32)).reshape(n_tok, HY_ORDER, 2, d)
    max_decay = math.log(HY_TARGET) / HY_FAST_DECAY
    min_decay = math.log(HY_TARGET) / HY_SLOW_DECAY
    deltas = jnp.abs(jnp.linspace(min_decay, max_decay, d, dtype=f32))
    window = jnp.exp(-t[:, :, None, None] * deltas) + HY_MOD_SHIFT
    return filt * window


def two_sided_long_conv(u, h_fwd, h_bwd, skip):
    n_tok, d = h_fwd.shape
    n_fft = 2 * n_tok
    k = jnp.concatenate([h_fwd, jnp.zeros((1, d), h_fwd.dtype), h_bwd[:0:-1]], axis=0)
    k_f = jnp.fft.rfft(k, n=n_fft, axis=0)
    u_f = jnp.fft.rfft(u.astype(jnp.float32), n=n_fft, axis=1)
    y = jnp.fft.irfft(u_f * k_f[None], n=n_fft, axis=1)[:, :n_tok]
    return (y + u.astype(jnp.float32) * skip.astype(jnp.float32)).astype(u.dtype)


def hyena_mixer(h, w_in, b_in, conv_w, conv_b, f_w1, f_b1, f_w2, f_b2, f_w3, freq, skip, w_out, b_out):
    n_tok = h.shape[1]
    u = short_conv(h @ w_in + b_in, conv_w, conv_b)
    parts = jnp.split(u, HY_ORDER + 1, axis=-1)
    filt = hyena_filters(n_tok, f_w1, f_b1, f_w2, f_b2, f_w3, freq)
    z = parts[0]
    for o in range(HY_ORDER):
        z = parts[o + 1] * two_sided_long_conv(z, filt[:, o, 0], filt[:, o, 1], skip[o])
    return z @ w_out + b_out


def moe_swiglu(h, router_w, w1, w3, w2):
    shape = h.shape
    hf = h.reshape(-1, shape[-1])
    n_tok = hf.shape[0]
    n_assign = n_tok * TOP_K
    logits = (hf @ router_w).astype(jnp.float32)
    top_logit, top_e = lax.top_k(logits, TOP_K)
    top_p = jax.nn.softmax(top_logit, axis=-1)
    flat_e = top_e.reshape(-1)
    order = jnp.argsort(flat_e)
    s_e = flat_e[order]
    s_tok = (order // TOP_K).astype(jnp.int32)
    s_w = top_p.reshape(-1)[order]
    counts = jnp.bincount(flat_e, length=N_EXPERTS)
    padded = (counts + MOE_BLOCK - 1) // MOE_BLOCK * MOE_BLOCK
    pad_end = jnp.cumsum(padded)
    pad_start = pad_end - padded
    start = jnp.cumsum(counts) - counts
    dest = pad_start[s_e] + jnp.arange(n_assign) - start[s_e]
    n_blocks = (n_assign + N_EXPERTS * (MOE_BLOCK - 1) + MOE_BLOCK - 1) // MOE_BLOCK
    n_rows = n_blocks * MOE_BLOCK
    buf_tok = jnp.zeros((n_rows,), jnp.int32).at[dest].set(s_tok)
    buf_w = jnp.zeros((n_rows,), jnp.float32).at[dest].set(s_w)
    block_e = jnp.minimum(jnp.searchsorted(pad_end, jnp.arange(n_blocks) * MOE_BLOCK, side='right'), N_EXPERTS - 1)
    xb = hf[buf_tok].reshape(n_blocks, MOE_BLOCK, shape[-1])
    yb = lax.map(lambda a: swiglu(a[0], w1[a[1]], w3[a[1]], w2[a[1]]), (xb, block_e))
    out = jnp.zeros(hf.shape, jnp.float32).at[buf_tok].add(
        yb.reshape(n_rows, -1).astype(jnp.float32) * buf_w[:, None])
    return out.astype(h.dtype).reshape(shape)


def setup_inputs(seed: int = 0) -> dict:
    key = jax.random.key(seed)
    ks = iter(jax.random.split(key, 40))
    D = D_MODEL

    def nrm(shape, scale):
        return scale * jax.random.normal(next(ks), shape, jnp.float32)

    return {
        'x': nrm((BATCH, SEQ, D), 1.0),
        'c': nrm((BATCH, D), 1.0),
        'ctx': nrm((BATCH, CTX_LEN, D), 1.0),
        'c_ctx': nrm((D,), 1.0),
        'mod_w': nrm((DEPTH, D, 6 * D), 0.5 * D ** -0.5),
        'mod_b': nrm((DEPTH, 6 * D), 0.02),
        'norm_g': 1.0 + nrm((DEPTH, 4, D), 0.02),
        'att_w_in': nrm((N_EVEN, D, IN_COLS), D ** -0.5),
        'att_w_out': nrm((N_EVEN, Q_COLS, D), Q_COLS ** -0.5),
        'att_q_norm_g': 1.0 + nrm((N_EVEN, HEAD_DIM), 0.02),
        'att_k_norm_g': 1.0 + nrm((N_EVEN, HEAD_DIM), 0.02),
        'na_rpb': nrm((N_EVEN, NA_HEADS, 2 * NA_WIN_ROWS - 1, 2 * NA_WIN_COLS - 1), 0.1),
        'ffn_w1': nrm((N_EVEN, D, D_FF), D ** -0.5),
        'ffn_w3': nrm((N_EVEN, D, D_FF), D ** -0.5),
        'ffn_w2': nrm((N_EVEN, D_FF, D), D_FF ** -0.5),
        'hy_w_in': nrm((N_ODD, D, (HY_ORDER + 1) * D), D ** -0.5),
        'hy_b_in': nrm((N_ODD, (HY_ORDER + 1) * D), 0.02),
        'hy_conv_w': nrm((N_ODD, HY_SHORT, (HY_ORDER + 1) * D), HY_SHORT ** -0.5),
        'hy_conv_b': nrm((N_ODD, (HY_ORDER + 1) * D), 0.02),
        'hy_f_w1': nrm((N_ODD, HY_EMB_DIM, HY_FILTER_HIDDEN), HY_EMB_DIM ** -0.5),
        'hy_f_b1': nrm((N_ODD, HY_FILTER_HIDDEN), 0.02),
        'hy_f_w2': nrm((N_ODD, HY_FILTER_HIDDEN, HY_FILTER_HIDDEN), HY_FILTER_HIDDEN ** -0.5),
        'hy_f_b2': nrm((N_ODD, HY_FILTER_HIDDEN), 0.02),
        'hy_f_w3': nrm((N_ODD, HY_FILTER_HIDDEN, 2 * HY_ORDER * D), 0.05 * HY_FILTER_HIDDEN ** -0.5),
        'hy_freq': 1.0 + nrm((N_ODD, HY_FILTER_HIDDEN), 0.02),
        'hy_skip': nrm((N_ODD, HY_ORDER, D), 0.1),
        'hy_w_out': nrm((N_ODD, D, D), D ** -0.5),
        'hy_b_out': nrm((N_ODD, D), 0.02),
        'moe_router': nrm((N_ODD, D, N_EXPERTS), D ** -0.5),
        'moe_w1': nrm((N_ODD, N_EXPERTS, D, D_FF), D ** -0.5),
        'moe_w3': nrm((N_ODD, N_EXPERTS, D, D_FF), D ** -0.5),
        'moe_w2': nrm((N_ODD, N_EXPERTS, D_FF, D), D_FF ** -0.5),
    }


def reference(x, c, ctx, c_ctx, mod_w, mod_b, norm_g, att_w_in, att_w_out, att_q_norm_g, att_k_norm_g,
              na_rpb, ffn_w1, ffn_w3, ffn_w2, hy_w_in, hy_b_in, hy_conv_w, hy_conv_b, hy_f_w1, hy_f_b1,
              hy_f_w2, hy_f_b2, hy_f_w3, hy_freq, hy_skip, hy_w_out, hy_b_out, moe_router, moe_w1,
              moe_w3, moe_w2):
    xc = ctx
    silu_c = jax.nn.silu(c)
    silu_cc = jax.nn.silu(c_ctx)
    for l in range(DEPTH):
        even = l % 2 == 0
        i = l // 2
        ctx_out = any(m % 2 == 0 for m in range(l + 1, DEPTH))
        ctx_in = even or ctx_out
        mod = (silu_c @ mod_w[l] + mod_b[l])[:, None, :]
        sh1, sc1, g1, sh2, sc2, g2 = jnp.split(mod, 6, axis=-1)
        h = modulate(x, norm_g[l, 0], sh1, sc1)
        hc = None
        if ctx_in:
            cmod = silu_cc @ mod_w[l] + mod_b[l]
            csh1, csc1, cg1, csh2, csc2, cg2 = jnp.split(cmod, 6)
            hc = modulate(xc, norm_g[l, 0], csh1, csc1)
        if even:
            y, yc = hybrid_attention(h, hc, att_w_in[i], att_w_out[i], att_q_norm_g[i],
                                     att_k_norm_g[i], na_rpb[i], ctx_out)
        else:
            hy = (hy_w_in[i], hy_b_in[i], hy_conv_w[i], hy_conv_b[i], hy_f_w1[i], hy_f_b1[i],
                  hy_f_w2[i], hy_f_b2[i], hy_f_w3[i], hy_freq[i], hy_skip[i], hy_w_out[i], hy_b_out[i])
            y = hyena_mixer(h, *hy)
            yc = hyena_mixer(hc, *hy) if ctx_out else None
        x = x + g1 * rms_norm(y, norm_g[l, 1])
        h = modulate(x, norm_g[l, 2], sh2, sc2)
        if even:
            y = swiglu(h, ffn_w1[i], ffn_w3[i], ffn_w2[i])
        else:
            y = moe_swiglu(h, moe_router[i], moe_w1[i], moe_w3[i], moe_w2[i])
        x = x + g2 * rms_norm(y, norm_g[l, 3])
        if ctx_out:
            xc = xc + cg1 * rms_norm(yc, norm_g[l, 1])
            hc = modulate(xc, norm_g[l, 2], csh2, csc2)
            if even:
                yc = swiglu(hc, ffn_w1[i], ffn_w3[i], ffn_w2[i])
            else:
                yc = moe_swiglu(hc, moe_router[i], moe_w1[i], moe_w3[i], moe_w2[i])
            xc = xc + cg2 * rms_norm(yc, norm_g[l, 3])
    return x
```

```python
import functools
import math

import jax
import jax.numpy as jnp
from jax import lax
from jax.experimental import pallas as pl
from jax.experimental.pallas import tpu as pltpu

F32 = jnp.float32
BF16 = jnp.bfloat16
HIGHEST = lax.Precision.HIGHEST

GRID_W = 64
HEAD_DIM = 128
NA_WIN_ROWS = 8
NA_WIN_COLS = 16
ROPE_BASE = 10000.0
HY_ORDER = 2
HY_SHORT = 3
HY_EMB_DIM = 33
HY_TARGET = 1e-2
HY_FAST_DECAY = 0.3
HY_SLOW_DECAY = 1.5
HY_MOD_SHIFT = 0.05
N_EXPERTS = 8
NORM_EPS = 1e-6
NEG_INF = -1e30

LANES = 128
V7X_VMEM_BYTES = 64 * 1024 * 1024
VMEM_LIMIT = 56 * 1024 * 1024

MOE_TILE = 512
HY_CHUNK = 512
NA_QROWS = 4
NA_KROWS = 12


def _cparams(sem):
    return pltpu.CompilerParams(dimension_semantics=sem, vmem_limit_bytes=VMEM_LIMIT)


def _weights_changed(be_ref, i):
    prev = be_ref[jnp.maximum(i - 1, 0)]
    return (i == 0) | (be_ref[i] != prev)


def _gmm_kernel(be_ref, a_ref, w_ref, b_ref, o_ref, wbf_ref, *, silu_in):
    i = pl.program_id(1)

    @pl.when(_weights_changed(be_ref, i))
    def _():
        wbf_ref[...] = w_ref[0].astype(BF16)

    a = a_ref[...]
    if silu_in:
        a = a.astype(F32)
        a = a * jax.nn.sigmoid(a)
    acc = jnp.dot(a.astype(BF16), wbf_ref[...], preferred_element_type=F32)
    o_ref[...] = (acc + b_ref[...]).astype(o_ref.dtype)


def gmm(a, w, block_e, bias=None, *, tm, tn, out_dtype, col_off=0, n_out=None, silu_in=False):
    m, k = a.shape
    n_out = w.shape[2] if n_out is None else n_out
    if bias is None:
        bias = jnp.zeros((1, w.shape[2]), F32)
    grid = (n_out // tn, m // tm)
    return pl.pallas_call(
        functools.partial(_gmm_kernel, silu_in=silu_in),
        out_shape=jax.ShapeDtypeStruct((m, n_out), out_dtype),
        grid_spec=pltpu.PrefetchScalarGridSpec(
            num_scalar_prefetch=1,
            grid=grid,
            in_specs=[
                pl.BlockSpec((tm, k), lambda j, i, be: (i, 0)),
                pl.BlockSpec((1, k, tn), lambda j, i, be: (be[i], 0, j + col_off)),
                pl.BlockSpec((1, tn), lambda j, i, be: (0, j + col_off)),
            ],
            out_specs=pl.BlockSpec((tm, tn), lambda j, i, be: (i, j)),
            scratch_shapes=[pltpu.VMEM((k, tn), BF16)],
        ),
        compiler_params=_cparams(("arbitrary", "arbitrary")),
        name="gmm",
    )(block_e, a, w, bias)


def _up_kernel(be_ref, a_ref, w1_ref, w3_ref, o_ref, w1bf_ref, w3bf_ref):
    i = pl.program_id(1)

    @pl.when(_weights_changed(be_ref, i))
    def _():
        w1bf_ref[...] = w1_ref[0].astype(BF16)
        w3bf_ref[...] = w3_ref[0].astype(BF16)

    a = a_ref[...]
    g = jnp.dot(a, w1bf_ref[...], preferred_element_type=F32)
    u = jnp.dot(a, w3bf_ref[...], preferred_element_type=F32)
    o_ref[...] = (g * jax.nn.sigmoid(g) * u).astype(o_ref.dtype)


def swiglu_up(a, w1, w3, block_e, *, tm, tn):
    m, k = a.shape
    n = w1.shape[2]
    grid = (n // tn, m // tm)
    return pl.pallas_call(
        _up_kernel,
        out_shape=jax.ShapeDtypeStruct((m, n), BF16),
        grid_spec=pltpu.PrefetchScalarGridSpec(
            num_scalar_prefetch=1,
            grid=grid,
            in_specs=[
                pl.BlockSpec((tm, k), lambda j, i, be: (i, 0)),
                pl.BlockSpec((1, k, tn), lambda j, i, be: (be[i], 0, j)),
                pl.BlockSpec((1, k, tn), lambda j, i, be: (be[i], 0, j)),
            ],
            out_specs=pl.BlockSpec((tm, tn), lambda j, i, be: (i, j)),
            scratch_shapes=[pltpu.VMEM((k, tn), BF16), pltpu.VMEM((k, tn), BF16)],
        ),
        compiler_params=_cparams(("arbitrary", "arbitrary")),
        name="swiglu_up",
    )(block_e, a, w1, w3)


def _rms(x, g):
    ms = jnp.mean(x * x, axis=-1, keepdims=True)
    return x * lax.rsqrt(ms + NORM_EPS) * g


def _mod_kernel(x_ref, g_ref, sh_ref, sc_ref, o_ref):
    y = _rms(x_ref[0], g_ref[...])
    o_ref[0] = (y * (1.0 + sc_ref[0]) + sh_ref[0]).astype(o_ref.dtype)


def modulate(x, g, shift, scale, *, out_dtype, tr=256):
    b, s, d = x.shape
    row = pl.BlockSpec((1, tr, d), lambda bi, si: (bi, si, 0))
    vec = pl.BlockSpec((1, 1, d), lambda bi, si: (bi, 0, 0))
    return pl.pallas_call(
        _mod_kernel,
        out_shape=jax.ShapeDtypeStruct((b, s, d), out_dtype),
        grid=(b, s // tr),
        in_specs=[row, pl.BlockSpec((1, d), lambda bi, si: (0, 0)), vec, vec],
        out_specs=row,
        compiler_params=_cparams(("parallel", "parallel")),
        name="modulate",
    )(x, g.reshape(1, d), shift, scale)


def _resid_kernel(x_ref, y_ref, g_ref, gate_ref, o_ref):
    y = _rms(y_ref[0].astype(F32), g_ref[...])
    o_ref[0] = x_ref[0] + gate_ref[0] * y


def residual(x, y, g, gate, *, tr=256):
    b, s, d = x.shape
    row = pl.BlockSpec((1, tr, d), lambda bi, si: (bi, si, 0))
    vec = pl.BlockSpec((1, 1, d), lambda bi, si: (bi, 0, 0))
    return pl.pallas_call(
        _resid_kernel,
        out_shape=jax.ShapeDtypeStruct((b, s, d), F32),
        grid=(b, s // tr),
        in_specs=[row, row, pl.BlockSpec((1, d), lambda bi, si: (0, 0)), vec],
        out_specs=row,
        compiler_params=_cparams(("parallel", "parallel")),
        name="residual",
    )(x, y, g.reshape(1, d), gate)


def _prep_kernel(x_ref, g_ref, cos_ref, sin_ref, o_ref, *, rope):
    y = _rms(x_ref[0].astype(F32), g_ref[...])
    if rope:
        rot = pltpu.roll(y, HEAD_DIM // 2, axis=1)
        y = y * cos_ref[...] + rot * sin_ref[...]
    o_ref[0] = y.astype(o_ref.dtype)


def head_prep(src, col0, n_heads, g, cos2, sin2, *, rope):
    b, s, _ = src.shape
    ts = min(s, 1024)
    return pl.pallas_call(
        functools.partial(_prep_kernel, rope=rope),
        out_shape=jax.ShapeDtypeStruct((b, s, n_heads * HEAD_DIM), BF16),
        grid=(b, s // ts, n_heads),
        in_specs=[
            pl.BlockSpec((1, ts, HEAD_DIM), lambda bi, si, h: (bi, si, col0 + h)),
            pl.BlockSpec((1, HEAD_DIM), lambda bi, si, h: (0, 0)),
            pl.BlockSpec((ts, HEAD_DIM), lambda bi, si, h: (si, 0)),
            pl.BlockSpec((ts, HEAD_DIM), lambda bi, si, h: (si, 0)),
        ],
        out_specs=pl.BlockSpec((1, ts, HEAD_DIM), lambda bi, si, h: (bi, si, h)),
        compiler_params=_cparams(("parallel", "parallel", "parallel")),
        name="head_prep",
    )(src, g.reshape(1, HEAD_DIM), cos2[:s], sin2[:s])


def _nt_dot(q, k):
    return lax.dot_general(q, k, (((1,), (1,)), ((), ())), preferred_element_type=F32)


def _softmax_pv(scores, values, out_dtype):
    m = scores[0].max(axis=-1, keepdims=True)
    for s in scores[1:]:
        m = jnp.maximum(m, s.max(axis=-1, keepdims=True))
    ps = [jnp.exp(s - m) for s in scores]
    l = ps[0].sum(axis=-1, keepdims=True)
    for p in ps[1:]:
        l = l + p.sum(axis=-1, keepdims=True)
    o = jnp.dot(ps[0].astype(BF16), values[0], preferred_element_type=F32)
    for p, v in zip(ps[1:], values[1:]):
        o = o + jnp.dot(p.astype(BF16), v, preferred_element_type=F32)
    return (o / l).astype(out_dtype)


def _scaled(q):
    return (q.astype(F32) * (HEAD_DIM ** -0.5)).astype(BF16)


def _attn_kernel(*refs, group, n_parts):
    q_ref = refs[0]
    k_refs = refs[1:1 + n_parts]
    v_refs = refs[1 + n_parts:1 + 2 * n_parts]
    o_ref = refs[1 + 2 * n_parts]
    ks = [r[0] for r in k_refs]
    vs = [r[0] for r in v_refs]
    for g in range(group):
        cols = slice(g * HEAD_DIM, (g + 1) * HEAD_DIM)
        q = _scaled(q_ref[0, :, cols])
        scores = [_nt_dot(q, k) for k in ks]
        o_ref[0, :, cols] = _softmax_pv(scores, vs, o_ref.dtype)


def full_attention(q_src, q_col0, n_kv, group, k_parts, v_parts, *, tq=256):
    b, sq, _ = q_src.shape
    tq = min(tq, sq)
    gw = group * HEAD_DIM
    assert (q_col0 * HEAD_DIM) % gw == 0
    q_blk0 = q_col0 * HEAD_DIM // gw
    in_specs = [pl.BlockSpec((1, tq, gw), lambda bi, h, qi: (bi, qi, q_blk0 + h))]
    args = [q_src]
    for arr, c0 in list(k_parts) + list(v_parts):
        in_specs.append(pl.BlockSpec((1, arr.shape[1], HEAD_DIM),
                                     functools.partial(lambda bi, h, qi, c0: (bi, 0, c0 + h), c0=c0)))
        args.append(arr)
    return pl.pallas_call(
        functools.partial(_attn_kernel, group=group, n_parts=len(k_parts)),
        out_shape=jax.ShapeDtypeStruct((b, sq, n_kv * gw), BF16),
        grid=(b, n_kv, sq // tq),
        in_specs=in_specs,
        out_specs=pl.BlockSpec((1, tq, gw), lambda bi, h, qi: (bi, qi, h)),
        compiler_params=_cparams(("parallel", "parallel", "parallel")),
        name="full_attention",
    )(*args)


def _na_kernel(q_ref, k_ref, v_ref, kc_ref, vc_ref, bias_ref, o_ref, *, rows):
    n_steps = rows // NA_QROWS
    kc = kc_ref[0]
    vc = vc_ref[0]
    for step in range(n_steps):
        r0 = step * NA_QROWS
        start = min(max(r0 - NA_WIN_ROWS // 2, 0), rows - NA_KROWS)
        pattern = 0 if step == 0 else (2 if step == n_steps - 1 else 1)
        qs = slice(r0 * GRID_W, (r0 + NA_QROWS) * GRID_W)
        kslice = slice(start * GRID_W, (start + NA_KROWS) * GRID_W)
        q = _scaled(q_ref[0, qs, :])
        s_lat = _nt_dot(q, k_ref[0, kslice, :]) + bias_ref[pattern, 0]
        s_ctx = _nt_dot(q, kc)
        o_ref[0, qs, :] = _softmax_pv([s_lat, s_ctx], [v_ref[0, kslice, :], vc], o_ref.dtype)


def na_bias_table(rpb, rows):
    n_steps = rows // NA_QROWS
    col = jnp.arange(GRID_W)
    col_start = jnp.clip(col - NA_WIN_COLS // 2, 0, GRID_W - NA_WIN_COLS)
    col_ok = (col[None, :] >= col_start[:, None]) & (col[None, :] < col_start[:, None] + NA_WIN_COLS)
    dc = jnp.clip(col[None, :] - col[:, None] + NA_WIN_COLS - 1, 0, 2 * NA_WIN_COLS - 2)
    rpb = rpb.astype(F32)
    tables = []
    for step in (0, 1, n_steps - 1):
        r0 = step * NA_QROWS
        start = min(max(r0 - NA_WIN_ROWS // 2, 0), rows - NA_KROWS)
        r = r0 + jnp.arange(NA_QROWS)
        rs = jnp.clip(r - NA_WIN_ROWS // 2, 0, rows - NA_WIN_ROWS)
        kr = start + jnp.arange(NA_KROWS)
        row_ok = (kr[None, :] >= rs[:, None]) & (kr[None, :] < rs[:, None] + NA_WIN_ROWS)
        dr = jnp.clip(kr[None, :] - r[:, None] + NA_WIN_ROWS - 1, 0, 2 * NA_WIN_ROWS - 2)
        vals = rpb[:, dr[:, None, :, None], dc[None, :, None, :]]
        ok = row_ok[:, None, :, None] & col_ok[None, :, None, :]
        bias = jnp.where(ok[None], vals, NEG_INF)
        tables.append(bias.reshape(rpb.shape[0], NA_QROWS * GRID_W, NA_KROWS * GRID_W))
    return jnp.stack(tables)


def na_attention(proj, projc, n_heads, q_col0, k_col0, v_col0, kc_col0, vc_col0, bias):
    b, s, _ = proj.shape
    n_ctx = projc.shape[1]
    rows = s // GRID_W
    lat = lambda c0: pl.BlockSpec((1, s, HEAD_DIM), lambda h, bi: (bi, 0, c0 + h))
    ctx = lambda c0: pl.BlockSpec((1, n_ctx, HEAD_DIM), lambda h, bi: (bi, 0, c0 + h))
    return pl.pallas_call(
        functools.partial(_na_kernel, rows=rows),
        out_shape=jax.ShapeDtypeStruct((b, s, n_heads * HEAD_DIM), BF16),
        grid=(n_heads, b),
        in_specs=[lat(q_col0), lat(k_col0), lat(v_col0), ctx(kc_col0), ctx(vc_col0),
                  pl.BlockSpec((3, 1) + bias.shape[2:], lambda h, bi: (0, h, 0, 0))],
        out_specs=pl.BlockSpec((1, s, HEAD_DIM), lambda h, bi: (bi, 0, h)),
        compiler_params=_cparams(("parallel", "parallel")),
        name="na_attention",
    )(proj, proj, proj, projc, projc, bias)


def rope_tables(n_tok):
    t = jnp.arange(n_tok)
    row = (t // GRID_W).astype(F32)
    col = (t % GRID_W).astype(F32)
    n_freq = HEAD_DIM // 4
    inv_freq = ROPE_BASE ** (-jnp.arange(n_freq, dtype=F32) / n_freq)
    ang = jnp.concatenate([row[:, None] * inv_freq, col[:, None] * inv_freq], axis=-1)
    cos, sin = jnp.cos(ang), jnp.sin(ang)
    return jnp.concatenate([cos, cos], axis=-1), jnp.concatenate([-sin, sin], axis=-1)


def _shortconv_kernel(u_ref, w_ref, b_ref, o_ref):
    u = u_ref[0].astype(F32)
    n = u.shape[0]
    row = lax.broadcasted_iota(jnp.int32, u.shape, 0)
    prev = jnp.where(row == 0, 0.0, pltpu.roll(u, 1, axis=0))
    nxt = jnp.where(row == n - 1, 0.0, pltpu.roll(u, n - 1, axis=0))
    o = prev * w_ref[0:1, :] + u * w_ref[1:2, :] + nxt * w_ref[2:3, :] + b_ref[...]
    o_ref[0] = o.astype(o_ref.dtype)


def short_conv(u, w, bias, *, tc=256):
    b, n, c = u.shape
    return pl.pallas_call(
        _shortconv_kernel,
        out_shape=jax.ShapeDtypeStruct((b, n, c), BF16),
        grid=(b, c // tc),
        in_specs=[pl.BlockSpec((1, n, tc), lambda bi, j: (bi, 0, j)),
                  pl.BlockSpec((HY_SHORT, tc), lambda bi, j: (0, j)),
                  pl.BlockSpec((1, tc), lambda bi, j: (0, j))],
        out_specs=pl.BlockSpec((1, n, tc), lambda bi, j: (bi, 0, j)),
        compiler_params=_cparams(("parallel", "parallel")),
        name="short_conv",
    )(u, w, bias.reshape(1, c))


def _filter_kernel(z_ref, w1_ref, b1_ref, w2_ref, b2_ref, fr_ref, w3f_ref, w3b_ref, dl_ref, o_ref, *, n_tok):
    z = z_ref[...]
    fr = fr_ref[...]
    a = jnp.sin(fr * (jnp.dot(z, w1_ref[...], precision=HIGHEST, preferred_element_type=F32) + b1_ref[...]))
    a = jnp.sin(fr * (jnp.dot(a, w2_ref[...], precision=HIGHEST, preferred_element_type=F32) + b2_ref[...]))
    f_fwd = jnp.dot(a, w3f_ref[...], precision=HIGHEST, preferred_element_type=F32)
    f_bwd = jnp.dot(a, w3b_ref[...], precision=HIGHEST, preferred_element_type=F32)
    window = jnp.exp(-z[:, 0:1] * dl_ref[...]) + HY_MOD_SHIFT
    lag_row = lax.broadcasted_iota(jnp.int32, f_fwd.shape, 0)
    filt = jnp.where(lag_row >= n_tok, f_fwd, f_bwd) * window
    o_ref[0] = jnp.where(lag_row == 0, 0.0, filt)


def hyena_lag_filters(n_tok, f_w1, f_b1, f_w2, f_b2, f_w3, freq, *, tc=256):
    d = f_w3.shape[1] // (2 * HY_ORDER)
    hidden = f_w1.shape[1]
    bands = (HY_EMB_DIM - 1) // 2
    pos = jnp.abs(jnp.arange(2 * n_tok) - n_tok)
    pos = jnp.minimum(pos, n_tok - 1)
    t = jnp.linspace(0.0, 1.0, n_tok, dtype=F32)[pos][:, None]
    w = (2.0 * math.pi * jnp.arange(n_tok, dtype=F32) / n_tok)[pos][:, None]
    f = jnp.linspace(1e-4, bands - 1, bands, dtype=F32)[None, :]
    z = jnp.concatenate([t, jnp.cos(f * w), -jnp.sin(f * w)], axis=-1)
    z = jnp.pad(z, ((0, 0), (0, LANES - HY_EMB_DIM)))
    w1 = jnp.pad(f_w1.astype(F32), ((0, LANES - HY_EMB_DIM), (0, 0)))
    max_decay = math.log(HY_TARGET) / HY_FAST_DECAY
    min_decay = math.log(HY_TARGET) / HY_SLOW_DECAY
    deltas = jnp.abs(jnp.linspace(min_decay, max_decay, d, dtype=F32)).reshape(1, d)
    nj = d // tc
    full = lambda shape: pl.BlockSpec(shape, lambda o, j: (0,) * len(shape))
    return pl.pallas_call(
        functools.partial(_filter_kernel, n_tok=n_tok),
        out_shape=jax.ShapeDtypeStruct((HY_ORDER, 2 * n_tok, d), F32),
        grid=(HY_ORDER, nj),
        in_specs=[full((2 * n_tok, LANES)), full((LANES, hidden)), full((1, hidden)),
                  full((hidden, hidden)), full((1, hidden)), full((1, hidden)),
                  pl.BlockSpec((hidden, tc), lambda o, j: (0, (2 * o) * nj + j)),
                  pl.BlockSpec((hidden, tc), lambda o, j: (0, (2 * o + 1) * nj + j)),
                  pl.BlockSpec((1, tc), lambda o, j: (0, j))],
        out_specs=pl.BlockSpec((1, 2 * n_tok, tc), lambda o, j: (o, 0, j)),
        compiler_params=_cparams(("parallel", "parallel")),
        name="hyena_filters",
    )(z, w1, f_b1.reshape(1, hidden).astype(F32), f_w2.astype(F32), f_b2.reshape(1, hidden).astype(F32),
      freq.reshape(1, hidden).astype(F32), f_w3.astype(F32), f_w3.astype(F32), deltas)


def _dft_angles(f, n, c):
    k = jnp.mod((2 * f + 1) * n, 4 * c).astype(F32)
    return k * (2.0 * math.pi / (4 * c))


def dft_matrices(c):
    f = jnp.arange(c)[:, None]
    ang_u = _dft_angles(f, jnp.arange(c)[None, :], c)
    fwd_u = jnp.concatenate([jnp.cos(ang_u), -jnp.sin(ang_u)], axis=0)
    ang_g = _dft_angles(f, jnp.arange(-c, c)[None, :], c)
    fwd_g = jnp.concatenate([jnp.cos(ang_g), -jnp.sin(ang_g)], axis=0)
    inv = jnp.concatenate([jnp.cos(ang_u).T, -jnp.sin(ang_u).T], axis=1) / c
    return fwd_u.astype(BF16), fwd_g.astype(BF16), inv.astype(BF16)


def _spectra_kernel(fa_ref, fb_ref, ka_ref, kb_ref, o_ref):
    def two_term(f, k):
        hi = k.astype(BF16)
        lo = (k - hi.astype(F32)).astype(BF16)
        return (jnp.dot(f, hi, preferred_element_type=F32) + jnp.dot(f, lo, preferred_element_type=F32))
    o_ref[0, 0] = two_term(fa_ref[...], ka_ref[0]) + two_term(fb_ref[...], kb_ref[0])


def filter_spectra(klag, fwd_g, c, *, tc=256):
    n_ord, two_l, d = klag.shape
    n_delta = two_l // c - 1
    fa, fb = fwd_g[:, :c], fwd_g[:, c:]
    return pl.pallas_call(
        _spectra_kernel,
        out_shape=jax.ShapeDtypeStruct((n_ord, n_delta, 2 * c, d), F32),
        grid=(n_ord, n_delta, d // tc),
        in_specs=[pl.BlockSpec((2 * c, c), lambda o, dl, j: (0, 0)),
                  pl.BlockSpec((2 * c, c), lambda o, dl, j: (0, 0)),
                  pl.BlockSpec((1, c, tc), lambda o, dl, j: (o, dl, j)),
                  pl.BlockSpec((1, c, tc), lambda o, dl, j: (o, dl + 1, j))],
        out_specs=pl.BlockSpec((1, 1, 2 * c, tc), lambda o, dl, j: (o, dl, 0, j)),
        compiler_params=_cparams(("parallel", "parallel", "parallel")),
        name="filter_spectra",
    )(fa, fb, klag, klag)


def _longconv_kernel(u_ref, gate_ref, sk_ref, skip_ref, fu_ref, inv_ref, o_ref, spec_ref, y_ref, *, c, n_ch, order):
    for j in range(n_ch):
        spec_ref[j] = jnp.dot(fu_ref[...], u_ref[0, j * c:(j + 1) * c, :], preferred_element_type=F32)
    skip = skip_ref[order:order + 1, :]
    for i in range(n_ch):
        top = None
        bot = None
        for j in range(n_ch):
            dl = i - j + n_ch - 1
            xr = spec_ref[j, :c, :]
            xi = spec_ref[j, c:, :]
            gr = sk_ref[0, dl, :c, :]
            gi = sk_ref[0, dl, c:, :]
            t = xr * gr - xi * gi
            b = xr * gi + xi * gr
            top = t if top is None else top + t
            bot = b if bot is None else bot + b
        y_ref[:c, :] = top.astype(BF16)
        y_ref[c:, :] = bot.astype(BF16)
        y = jnp.dot(inv_ref[...], y_ref[...], preferred_element_type=F32)
        rows = slice(i * c, (i + 1) * c)
        u = u_ref[0, rows, :].astype(F32)
        o_ref[0, rows, :] = (gate_ref[0, rows, :].astype(F32) * (y + u * skip)).astype(o_ref.dtype)


def long_conv_gate(u_src, u_col0, gate_src, gate_col0, spectra, order, skip, fwd_u, inv, c, *, tc=256):
    b, n, _ = u_src.shape
    d = spectra.shape[3]
    n_ch = n // c
    n_delta = spectra.shape[1]
    nj = d // tc
    return pl.pallas_call(
        functools.partial(_longconv_kernel, c=c, n_ch=n_ch, order=order),
        out_shape=jax.ShapeDtypeStruct((b, n, d), BF16),
        grid=(nj, b),
        in_specs=[pl.BlockSpec((1, n, tc), lambda j, bi: (bi, 0, u_col0 * nj + j)),
                  pl.BlockSpec((1, n, tc), lambda j, bi: (bi, 0, gate_col0 * nj + j)),
                  pl.BlockSpec((1, n_delta, 2 * c, tc), lambda j, bi: (order, 0, 0, j)),
                  pl.BlockSpec((HY_ORDER, tc), lambda j, bi: (0, j)),
                  pl.BlockSpec((2 * c, c), lambda j, bi: (0, 0)),
                  pl.BlockSpec((c, 2 * c), lambda j, bi: (0, 0))],
        out_specs=pl.BlockSpec((1, n, tc), lambda j, bi: (bi, 0, j)),
        scratch_shapes=[pltpu.VMEM((n_ch, 2 * c, tc), F32), pltpu.VMEM((2 * c, tc), BF16)],
        compiler_params=_cparams(("parallel", "parallel")),
        name="long_conv_gate",
    )(u_src, gate_src, spectra, skip.astype(F32), fwd_u, inv)


def _router_kernel(h_ref, w_ref, o_ref):
    logits = jnp.dot(h_ref[...], w_ref[...], precision=HIGHEST, preferred_element_type=F32)
    lane = lax.broadcasted_iota(jnp.int32, logits.shape, 1).astype(F32)
    lg = jnp.where(lane < N_EXPERTS, logits, -jnp.inf)
    m1 = lg.max(axis=-1, keepdims=True)
    e1 = jnp.where(lg == m1, lane, float(LANES)).min(axis=-1, keepdims=True)
    lg2 = jnp.where(lane == e1, -jnp.inf, lg)
    m2 = lg2.max(axis=-1, keepdims=True)
    e2 = jnp.where(lg2 == m2, lane, float(LANES)).min(axis=-1, keepdims=True)
    t = jnp.exp(m2 - m1)
    p1 = 1.0 / (1.0 + t)
    p2 = t / (1.0 + t)
    o_ref[...] = jnp.where(lane == 0, e1, jnp.where(lane == 1, e2, jnp.where(lane == 2, p1,
                           jnp.where(lane == 3, p2, 0.0))))


def moe_router(h, router_w, *, tm=512):
    t, d = h.shape
    w = jnp.pad(router_w.astype(F32), ((0, 0), (0, LANES - N_EXPERTS)))
    return pl.pallas_call(
        _router_kernel,
        out_shape=jax.ShapeDtypeStruct((t, LANES), F32),
        grid=(t // tm,),
        in_specs=[pl.BlockSpec((tm, d), lambda i: (i, 0)), pl.BlockSpec((d, LANES), lambda i: (0, 0))],
        out_specs=pl.BlockSpec((tm, LANES), lambda i: (i, 0)),
        compiler_params=_cparams(("parallel",)),
        name="moe_router",
    )(h, w)


def _row_copy(src_ref, dst_ref, sem, src_row, dst_row):
    return pltpu.make_async_copy(src_ref.at[pl.ds(src_row, 1)], dst_ref.at[pl.ds(dst_row, 1)], sem)


def _gather_rows(idx_vmem_ref, idx_smem, src_ref, buf_ref, sem, isem, n):
    cp = pltpu.make_async_copy(idx_vmem_ref, idx_smem, isem)
    cp.start()
    cp.wait()

    def issue(r, carry):
        _row_copy(src_ref, buf_ref, sem, idx_smem[0, r], r).start()
        return carry

    lax.fori_loop(0, n, issue, 0)

    def drain(r, carry):
        _row_copy(src_ref, buf_ref, sem, 0, r).wait()
        return carry

    lax.fori_loop(0, n, drain, 0)


def _dispatch_kernel(idx_ref, src_ref, o_ref, idx_smem, buf_ref, sem, isem, *, tm):
    _gather_rows(idx_ref.at[0], idx_smem, src_ref, buf_ref, sem, isem, tm)
    o_ref[...] = buf_ref[...].astype(o_ref.dtype)


def moe_dispatch(h, buf_tok, *, tm=MOE_TILE):
    _, d = h.shape
    n_rows = buf_tok.shape[0]
    idx = buf_tok.reshape(n_rows // tm, 1, tm)
    return pl.pallas_call(
        functools.partial(_dispatch_kernel, tm=tm),
        out_shape=jax.ShapeDtypeStruct((n_rows, d), BF16),
        grid=(n_rows // tm,),
        in_specs=[pl.BlockSpec((1, 1, tm), lambda i: (i, 0, 0)), pl.BlockSpec(memory_space=pl.ANY)],
        out_specs=pl.BlockSpec((tm, d), lambda i: (i, 0)),
        scratch_shapes=[pltpu.SMEM((1, tm), jnp.int32), pltpu.VMEM((tm, d), F32),
                        pltpu.SemaphoreType.DMA, pltpu.SemaphoreType.DMA],
        compiler_params=_cparams(("arbitrary",)),
        name="moe_dispatch",
    )(idx, h)


def _combine_kernel(i0_ref, i1_ref, yb_ref, slab_ref, o_ref, idx_smem, buf0_ref, buf1_ref, sem, isem, *, tm):
    _gather_rows(i0_ref.at[0], idx_smem, yb_ref, buf0_ref, sem, isem, tm)
    _gather_rows(i1_ref.at[0], idx_smem, yb_ref, buf1_ref, sem, isem, tm)
    o_ref[...] = buf0_ref[...] * slab_ref[:, 2:3] + buf1_ref[...] * slab_ref[:, 3:4]


def moe_combine(yb, pos0, pos1, slab, *, tm=256):
    t = pos0.shape[0]
    d = yb.shape[1]
    idx_spec = pl.BlockSpec((1, 1, tm), lambda i: (i, 0, 0))
    return pl.pallas_call(
        functools.partial(_combine_kernel, tm=tm),
        out_shape=jax.ShapeDtypeStruct((t, d), F32),
        grid=(t // tm,),
        in_specs=[idx_spec, idx_spec, pl.BlockSpec(memory_space=pl.ANY),
                  pl.BlockSpec((tm, LANES), lambda i: (i, 0))],
        out_specs=pl.BlockSpec((tm, d), lambda i: (i, 0)),
        scratch_shapes=[pltpu.SMEM((1, tm), jnp.int32), pltpu.VMEM((tm, d), F32), pltpu.VMEM((tm, d), F32),
                        pltpu.SemaphoreType.DMA, pltpu.SemaphoreType.DMA],
        compiler_params=_cparams(("arbitrary",)),
        name="moe_combine",
    )(pos0.reshape(t // tm, 1, tm), pos1.reshape(t // tm, 1, tm), yb, slab)


def moe_routing_tables(slab, tile):
    t = slab.shape[0]
    flat_e = slab[:, :2].astype(jnp.int32).reshape(-1)
    n_assign = flat_e.shape[0]
    onehot = (flat_e[:, None] == jnp.arange(N_EXPERTS)[None, :]).astype(jnp.int32)
    csum = jnp.cumsum(onehot, axis=0)
    counts = csum[-1]
    rank = jnp.take_along_axis(csum, flat_e[:, None], axis=1)[:, 0] - 1
    padded = (counts + tile - 1) // tile * tile
    pad_end = jnp.cumsum(padded)
    pad_start = pad_end - padded
    dest = (pad_start[flat_e] + rank).astype(jnp.int32)
    n_blocks = (n_assign + N_EXPERTS * (tile - 1) + tile - 1) // tile
    buf_tok = jnp.zeros((n_blocks * tile,), jnp.int32).at[dest].set(jnp.arange(n_assign, dtype=jnp.int32) // 2)
    block_e = jnp.minimum(jnp.searchsorted(pad_end, jnp.arange(n_blocks) * tile, side='right'),
                          N_EXPERTS - 1).astype(jnp.int32)
    dest = dest.reshape(t, 2)
    return buf_tok, block_e, dest[:, 0], dest[:, 1]


def moe_swiglu(h, router_w, w1, w3, w2, layer):
    slab = moe_router(h, router_w)
    buf_tok, block_e, pos0, pos1 = moe_routing_tables(slab, MOE_TILE)
    block_e = block_e + layer * N_EXPERTS
    xb = moe_dispatch(h, buf_tok)
    mid = swiglu_up(xb, w1, w3, block_e, tm=MOE_TILE, tn=512)
    yb = gmm(mid, w2, block_e, tm=MOE_TILE, tn=512, out_dtype=F32)
    return moe_combine(yb, pos0, pos1, slab)


def _row_tile(k):
    return 1024 if k <= 2048 else 512


def _dense_ids(n_rows, tm, idx):
    return jnp.full((n_rows // tm,), idx, jnp.int32)


def _dense(a3, w, idx, *, bias=None, out_dtype=BF16, tn=512, col_off=0, n_out=None):
    b, s, k = a3.shape
    m = b * s
    tm = min(_row_tile(k), m)
    out = gmm(a3.reshape(m, k), w, _dense_ids(m, tm, idx), bias, tm=tm, tn=tn, out_dtype=out_dtype,
              col_off=col_off, n_out=n_out)
    return out.reshape(b, s, -1)


def _dense_swiglu(h3, w1, w3, w2, idx):
    b, s, k = h3.shape
    m = b * s
    tm = min(1024, m)
    mid = swiglu_up(h3.reshape(m, k), w1, w3, _dense_ids(m, tm, idx), tm=tm, tn=512)
    tm2 = min(_row_tile(mid.shape[1]), m)
    return gmm(mid, w2, _dense_ids(m, tm2, idx), tm=tm2, tn=512, out_dtype=BF16).reshape(b, s, -1)


def _attention_layer(h, hc, i, w_in, w_out, q_g, k_g, bias, cos2, sin2, ctx_out):
    d = h.shape[-1]
    n_na = d // (2 * HEAD_DIM)
    n_q = n_na
    n_kv = max(1, n_q // 4)
    group = n_q // n_kv
    c_bq, c_ak, c_av = n_na, n_na + n_q, 2 * n_na + n_q
    c_bk, c_bv = 3 * n_na + n_q, 3 * n_na + n_q + n_kv
    q_cols = (n_na + n_q) * HEAD_DIM
    proj = _dense(h, w_in, i)
    if ctx_out:
        projc = _dense(hc, w_in, i)
        shift = 0
    else:
        projc = _dense(hc, w_in, i, col_off=q_cols // 512, n_out=w_in.shape[2] - q_cols)
        shift = n_na + n_q
    qb = head_prep(proj, c_bq, n_q, q_g, cos2, sin2, rope=True)
    kb = head_prep(proj, c_bk, n_kv, k_g, cos2, sin2, rope=True)
    kbc = head_prep(projc, c_bk - shift, n_kv, k_g, cos2, sin2, rope=False)
    o_a = na_attention(proj, projc, n_na, 0, c_ak, c_av, c_ak - shift, c_av - shift, bias)
    o_b = full_attention(qb, 0, n_kv, group, [(kbc, 0), (kb, 0)], [(projc, c_bv - shift), (proj, c_bv)])
    y = _dense(jnp.concatenate([o_a, o_b], axis=-1), w_out, i)
    if not ctx_out:
        return y, None
    qbc = head_prep(projc, c_bq, n_q, q_g, cos2, sin2, rope=False)
    oc_a = full_attention(projc, 0, n_na, 1, [(projc, c_ak)], [(projc, c_av)])
    oc_b = full_attention(qbc, 0, n_kv, group, [(kbc, 0)], [(projc, c_bv)])
    yc = _dense(jnp.concatenate([oc_a, oc_b], axis=-1), w_out, i)
    return y, yc


def _hyena_layer(h, i, p):
    (w_in, b_in, conv_w, conv_b, f_w1, f_b1, f_w2, f_b2, f_w3, freq, skip, w_out, b_out) = p
    n_tok = h.shape[1]
    c = min(HY_CHUNK, n_tok)
    u0 = _dense(h, w_in, i, bias=b_in[i].reshape(1, -1))
    u = short_conv(u0, conv_w[i], conv_b[i])
    fwd_u, fwd_g, inv = dft_matrices(c)
    klag = hyena_lag_filters(n_tok, f_w1[i], f_b1[i], f_w2[i], f_b2[i], f_w3[i], freq[i])
    spectra = filter_spectra(klag, fwd_g, c)
    z = long_conv_gate(u, 0, u, 1, spectra, 0, skip[i], fwd_u, inv, c)
    z = long_conv_gate(z, 0, u, 2, spectra, 1, skip[i], fwd_u, inv, c)
    return _dense(z, w_out, i, bias=b_out[i].reshape(1, -1))


def kernel(x, c, ctx, c_ctx, mod_w, mod_b, norm_g, att_w_in, att_w_out, att_q_norm_g, att_k_norm_g, na_rpb,
           ffn_w1, ffn_w3, ffn_w2, hy_w_in, hy_b_in, hy_conv_w, hy_conv_b, hy_f_w1, hy_f_b1, hy_f_w2, hy_f_b2,
           hy_f_w3, hy_freq, hy_skip, hy_w_out, hy_b_out, moe_router_w, moe_w1, moe_w3, moe_w2):
    depth = mod_w.shape[0]
    b, s, d = x.shape
    n_ctx = ctx.shape[1]
    d_ff = moe_w1.shape[-1]
    cos2, sin2 = rope_tables(s)
    moe_w1f = moe_w1.reshape(-1, d, d_ff)
    moe_w3f = moe_w3.reshape(-1, d, d_ff)
    moe_w2f = moe_w2.reshape(-1, d_ff, d)
    hy = (hy_w_in, hy_b_in, hy_conv_w, hy_conv_b, hy_f_w1, hy_f_b1, hy_f_w2, hy_f_b2, hy_f_w3, hy_freq,
          hy_skip, hy_w_out, hy_b_out)

    n_mod_rows = 16
    cond = jnp.concatenate([c, c_ctx[None, :], jnp.zeros((n_mod_rows - b - 1, d), F32)], axis=0)

    xc = ctx
    for l in range(depth):
        even = l % 2 == 0
        i = l // 2
        ctx_out = any(m % 2 == 0 for m in range(l + 1, depth))
        ctx_in = even or ctx_out
        mod = gmm(cond, mod_w, _dense_ids(n_mod_rows, n_mod_rows, l), mod_b[l].reshape(1, -1),
                  tm=n_mod_rows, tn=512, out_dtype=F32, silu_in=True)
        m_lat = mod[:b].reshape(b, 6, 1, d)
        sh1, sc1, g1, sh2, sc2, g2 = (m_lat[:, k] for k in range(6))
        m_ctx = jnp.broadcast_to(mod[b].reshape(1, 6, 1, d), (b, 6, 1, d))
        csh1, csc1, cg1, csh2, csc2, cg2 = (m_ctx[:, k] for k in range(6))

        h = modulate(x, norm_g[l, 0], sh1, sc1, out_dtype=BF16)
        hc = modulate(xc, norm_g[l, 0], csh1, csc1, out_dtype=BF16) if ctx_in else None
        if even:
            bias = na_bias_table(na_rpb[i], s // GRID_W)
            y, yc = _attention_layer(h, hc, i, att_w_in, att_w_out, att_q_norm_g[i], att_k_norm_g[i], bias,
                                     cos2, sin2, ctx_out)
        else:
            y = _hyena_layer(h, i, hy)
            yc = _hyena_layer(hc, i, hy) if ctx_out else None
        x = residual(x, y, norm_g[l, 1], g1)
        if even:
            h = modulate(x, norm_g[l, 2], sh2, sc2, out_dtype=BF16)
            y = _dense_swiglu(h, ffn_w1, ffn_w3, ffn_w2, i)
        else:
            h = modulate(x, norm_g[l, 2], sh2, sc2, out_dtype=F32)
            y = moe_swiglu(h.reshape(b * s, d), moe_router_w[i], moe_w1f, moe_w3f, moe_w2f, i).reshape(b, s, d)
        x = residual(x, y, norm_g[l, 3], g2)
        if ctx_out:
            xc = residual(xc, yc, norm_g[l, 1], cg1)
            if even:
                hc = modulate(xc, norm_g[l, 2], csh2, csc2, out_dtype=BF16)
                yc = _dense_swiglu(hc, ffn_w1, ffn_w3, ffn_w2, i)
            else:
                hc = modulate(xc, norm_g[l, 2], csh2, csc2, out_dtype=F32)
                yc = moe_swiglu(hc.reshape(b * n_ctx, d), moe_router_w[i], moe_w1f, moe_w3f, moe_w2f,
                                i).reshape(b, n_ctx, d)
            xc = residual(xc, yc, norm_g[l, 3], cg2)
    return x
```

```python
import functools
import math

import jax
import jax.numpy as jnp
import numpy as np
from jax import lax
from jax.experimental import pallas as pl
from jax.experimental.pallas import tpu as pltpu

F32 = jnp.float32
BF16 = jnp.bfloat16
HIGHEST = lax.Precision.HIGHEST

GRID_W = 64
HEAD_DIM = 128
NA_WIN_ROWS = 8
NA_WIN_COLS = 16
ROPE_BASE = 10000.0
HY_ORDER = 2
HY_SHORT = 3
HY_EMB_DIM = 33
HY_TARGET = 1e-2
HY_FAST_DECAY = 0.3
HY_SLOW_DECAY = 1.5
HY_MOD_SHIFT = 0.05
N_EXPERTS = 8
NORM_EPS = 1e-6
NEG_INF = -1e30

LANES = 128
V7X_VMEM_BYTES = 64 * 1024 * 1024
VMEM_LIMIT = 56 * 1024 * 1024

MOE_TILE = 512
GATHER_UNROLL = 8
HY_CHUNK = 512
NA_QROWS = 4
NA_KROWS = 12


def _cparams(sem):
    return pltpu.CompilerParams(dimension_semantics=sem, vmem_limit_bytes=VMEM_LIMIT)


def _weights_changed(be_ref, i):
    prev = be_ref[jnp.maximum(i - 1, 0)]
    return (i == 0) | (be_ref[i] != prev)


def _gmm_kernel(be_ref, nu_ref, a_ref, w_ref, b_ref, o_ref, wbf_ref, *, silu_in):
    i = pl.program_id(1)
    used = i < nu_ref[0]

    @pl.when(_weights_changed(be_ref, i))
    def _():
        wbf_ref[...] = w_ref[0].astype(BF16)

    @pl.when(used)
    def _():
        a = a_ref[...]
        if silu_in:
            a = a.astype(F32)
            a = a * jax.nn.sigmoid(a)
        acc = jnp.dot(a.astype(BF16), wbf_ref[...], preferred_element_type=F32)
        o_ref[...] = (acc + b_ref[...]).astype(o_ref.dtype)

    @pl.when(jnp.logical_not(used))
    def _():
        o_ref[...] = jnp.zeros_like(o_ref)


def _all_used(block_e):
    return jnp.full((1,), block_e.shape[0], jnp.int32)


def gmm(a, w, block_e, bias=None, *, tm, tn, out_dtype, col_off=0, n_out=None, silu_in=False, n_used=None):
    m, k = a.shape
    n_out = w.shape[2] if n_out is None else n_out
    if bias is None:
        bias = jnp.zeros((1, w.shape[2]), F32)
    n_used = _all_used(block_e) if n_used is None else n_used
    grid = (n_out // tn, m // tm)
    return pl.pallas_call(
        functools.partial(_gmm_kernel, silu_in=silu_in),
        out_shape=jax.ShapeDtypeStruct((m, n_out), out_dtype),
        grid_spec=pltpu.PrefetchScalarGridSpec(
            num_scalar_prefetch=2,
            grid=grid,
            in_specs=[
                pl.BlockSpec((tm, k), lambda j, i, be, nu: (jnp.minimum(i, nu[0] - 1), 0)),
                pl.BlockSpec((1, k, tn), lambda j, i, be, nu: (be[i], 0, j + col_off)),
                pl.BlockSpec((1, tn), lambda j, i, be, nu: (0, j + col_off)),
            ],
            out_specs=pl.BlockSpec((tm, tn), lambda j, i, be, nu: (i, j)),
            scratch_shapes=[pltpu.VMEM((k, tn), BF16)],
        ),
        compiler_params=_cparams(("arbitrary", "arbitrary")),
        name="gmm",
    )(block_e, n_used, a, w, bias)


def _up_kernel(be_ref, nu_ref, a_ref, w1_ref, w3_ref, o_ref, w1bf_ref, w3bf_ref):
    i = pl.program_id(1)
    used = i < nu_ref[0]

    @pl.when(_weights_changed(be_ref, i))
    def _():
        w1bf_ref[...] = w1_ref[0].astype(BF16)
        w3bf_ref[...] = w3_ref[0].astype(BF16)

    @pl.when(used)
    def _():
        a = a_ref[...]
        g = jnp.dot(a, w1bf_ref[...], preferred_element_type=F32)
        u = jnp.dot(a, w3bf_ref[...], preferred_element_type=F32)
        o_ref[...] = (g * jax.nn.sigmoid(g) * u).astype(o_ref.dtype)

    @pl.when(jnp.logical_not(used))
    def _():
        o_ref[...] = jnp.zeros_like(o_ref)


def swiglu_up(a, w1, w3, block_e, *, tm, tn, n_used=None):
    m, k = a.shape
    n = w1.shape[2]
    n_used = _all_used(block_e) if n_used is None else n_used
    grid = (n // tn, m // tm)
    return pl.pallas_call(
        _up_kernel,
        out_shape=jax.ShapeDtypeStruct((m, n), BF16),
        grid_spec=pltpu.PrefetchScalarGridSpec(
            num_scalar_prefetch=2,
            grid=grid,
            in_specs=[
                pl.BlockSpec((tm, k), lambda j, i, be, nu: (jnp.minimum(i, nu[0] - 1), 0)),
                pl.BlockSpec((1, k, tn), lambda j, i, be, nu: (be[i], 0, j)),
                pl.BlockSpec((1, k, tn), lambda j, i, be, nu: (be[i], 0, j)),
            ],
            out_specs=pl.BlockSpec((tm, tn), lambda j, i, be, nu: (i, j)),
            scratch_shapes=[pltpu.VMEM((k, tn), BF16), pltpu.VMEM((k, tn), BF16)],
        ),
        compiler_params=_cparams(("arbitrary", "arbitrary")),
        name="swiglu_up",
    )(block_e, n_used, a, w1, w3)


def _rms(x, g):
    ms = jnp.mean(x * x, axis=-1, keepdims=True)
    return x * lax.rsqrt(ms + NORM_EPS) * g


def _mod_kernel(x_ref, g_ref, sh_ref, sc_ref, o_ref):
    y = _rms(x_ref[0], g_ref[...])
    o_ref[0] = (y * (1.0 + sc_ref[0]) + sh_ref[0]).astype(o_ref.dtype)


def modulate(x, g, shift, scale, *, out_dtype, tr=256):
    b, s, d = x.shape
    row = pl.BlockSpec((1, tr, d), lambda bi, si: (bi, si, 0))
    vec = pl.BlockSpec((1, 1, d), lambda bi, si: (bi, 0, 0))
    return pl.pallas_call(
        _mod_kernel,
        out_shape=jax.ShapeDtypeStruct((b, s, d), out_dtype),
        grid=(b, s // tr),
        in_specs=[row, pl.BlockSpec((1, d), lambda bi, si: (0, 0)), vec, vec],
        out_specs=row,
        compiler_params=_cparams(("parallel", "parallel")),
        name="modulate",
    )(x, g.reshape(1, d), shift, scale)


def _resid_kernel(x_ref, y_ref, g_ref, gate_ref, o_ref):
    y = _rms(y_ref[0].astype(F32), g_ref[...])
    o_ref[0] = x_ref[0] + gate_ref[0] * y


def residual(x, y, g, gate, *, tr=256):
    b, s, d = x.shape
    row = pl.BlockSpec((1, tr, d), lambda bi, si: (bi, si, 0))
    vec = pl.BlockSpec((1, 1, d), lambda bi, si: (bi, 0, 0))
    return pl.pallas_call(
        _resid_kernel,
        out_shape=jax.ShapeDtypeStruct((b, s, d), F32),
        grid=(b, s // tr),
        in_specs=[row, row, pl.BlockSpec((1, d), lambda bi, si: (0, 0)), vec],
        out_specs=row,
        compiler_params=_cparams(("parallel", "parallel")),
        name="residual",
    )(x, y, g.reshape(1, d), gate)


def _resid_mod_kernel(x_ref, y_ref, gy_ref, gate_ref, gh_ref, sh_ref, sc_ref, xo_ref, ho_ref):
    x = x_ref[0] + gate_ref[0] * _rms(y_ref[0].astype(F32), gy_ref[...])
    xo_ref[0] = x
    ho_ref[0] = (_rms(x, gh_ref[...]) * (1.0 + sc_ref[0]) + sh_ref[0]).astype(ho_ref.dtype)


def residual_modulate(x, y, g_y, gate, g_h, shift, scale, *, out_dtype, tr=256):
    b, s, d = x.shape
    row = pl.BlockSpec((1, tr, d), lambda bi, si: (bi, si, 0))
    vec = pl.BlockSpec((1, 1, d), lambda bi, si: (bi, 0, 0))
    gain = pl.BlockSpec((1, d), lambda bi, si: (0, 0))
    return pl.pallas_call(
        _resid_mod_kernel,
        out_shape=(jax.ShapeDtypeStruct((b, s, d), F32), jax.ShapeDtypeStruct((b, s, d), out_dtype)),
        grid=(b, s // tr),
        in_specs=[row, row, gain, vec, gain, vec, vec],
        out_specs=(row, row),
        compiler_params=_cparams(("parallel", "parallel")),
        name="residual_modulate",
    )(x, y, g_y.reshape(1, d), gate, g_h.reshape(1, d), shift, scale)


def _prep_kernel(x_ref, g_ref, cos_ref, sin_ref, o_ref, *, rope):
    y = _rms(x_ref[0].astype(F32), g_ref[...])
    if rope:
        rot = pltpu.roll(y, HEAD_DIM // 2, axis=1)
        y = y * cos_ref[...] + rot * sin_ref[...]
    o_ref[0] = y.astype(o_ref.dtype)


def head_prep(src, col0, n_heads, g, cos2, sin2, *, rope):
    b, s, _ = src.shape
    ts = min(s, 1024)
    return pl.pallas_call(
        functools.partial(_prep_kernel, rope=rope),
        out_shape=jax.ShapeDtypeStruct((b, s, n_heads * HEAD_DIM), BF16),
        grid=(b, s // ts, n_heads),
        in_specs=[
            pl.BlockSpec((1, ts, HEAD_DIM), lambda bi, si, h: (bi, si, col0 + h)),
            pl.BlockSpec((1, HEAD_DIM), lambda bi, si, h: (0, 0)),
            pl.BlockSpec((ts, HEAD_DIM), lambda bi, si, h: (si, 0)),
            pl.BlockSpec((ts, HEAD_DIM), lambda bi, si, h: (si, 0)),
        ],
        out_specs=pl.BlockSpec((1, ts, HEAD_DIM), lambda bi, si, h: (bi, si, h)),
        compiler_params=_cparams(("parallel", "parallel", "parallel")),
        name="head_prep",
    )(src, g.reshape(1, HEAD_DIM), cos2[:s], sin2[:s])


def _nt_dot(q, k):
    return lax.dot_general(q, k, (((1,), (1,)), ((), ())), preferred_element_type=F32)


def _softmax_pv(scores, values, out_dtype):
    m = scores[0].max(axis=-1, keepdims=True)
    for s in scores[1:]:
        m = jnp.maximum(m, s.max(axis=-1, keepdims=True))
    ps = [jnp.exp(s - m) for s in scores]
    l = ps[0].sum(axis=-1, keepdims=True)
    for p in ps[1:]:
        l = l + p.sum(axis=-1, keepdims=True)
    o = jnp.dot(ps[0].astype(BF16), values[0], preferred_element_type=F32)
    for p, v in zip(ps[1:], values[1:]):
        o = o + jnp.dot(p.astype(BF16), v, preferred_element_type=F32)
    return (o / l).astype(out_dtype)


def _scaled(q):
    return (q.astype(F32) * (HEAD_DIM ** -0.5)).astype(BF16)


def _attn_kernel(*refs, group, n_parts):
    q_ref = refs[0]
    k_refs = refs[1:1 + n_parts]
    v_refs = refs[1 + n_parts:1 + 2 * n_parts]
    o_ref = refs[1 + 2 * n_parts]
    ks = [r[0] for r in k_refs]
    vs = [r[0] for r in v_refs]
    for g in range(group):
        cols = slice(g * HEAD_DIM, (g + 1) * HEAD_DIM)
        q = _scaled(q_ref[0, :, cols])
        scores = [_nt_dot(q, k) for k in ks]
        o_ref[0, :, cols] = _softmax_pv(scores, vs, o_ref.dtype)


def full_attention(q_src, q_col0, n_kv, group, k_parts, v_parts, *, tq=256):
    b, sq, _ = q_src.shape
    tq = min(tq, sq)
    gw = group * HEAD_DIM
    assert (q_col0 * HEAD_DIM) % gw == 0
    q_blk0 = q_col0 * HEAD_DIM // gw
    in_specs = [pl.BlockSpec((1, tq, gw), lambda bi, h, qi: (bi, qi, q_blk0 + h))]
    args = [q_src]
    for arr, c0 in list(k_parts) + list(v_parts):
        in_specs.append(pl.BlockSpec((1, arr.shape[1], HEAD_DIM),
                                     functools.partial(lambda bi, h, qi, c0: (bi, 0, c0 + h), c0=c0)))
        args.append(arr)
    return pl.pallas_call(
        functools.partial(_attn_kernel, group=group, n_parts=len(k_parts)),
        out_shape=jax.ShapeDtypeStruct((b, sq, n_kv * gw), BF16),
        grid=(b, n_kv, sq // tq),
        in_specs=in_specs,
        out_specs=pl.BlockSpec((1, tq, gw), lambda bi, h, qi: (bi, qi, h)),
        compiler_params=_cparams(("parallel", "parallel", "parallel")),
        name="full_attention",
    )(*args)


def _na_kernel(q_ref, k_ref, v_ref, kc_ref, vc_ref, bias_ref, o_ref, *, rows):
    n_steps = rows // NA_QROWS
    kc = kc_ref[0]
    vc = vc_ref[0]
    for step in range(n_steps):
        r0 = step * NA_QROWS
        start = min(max(r0 - NA_WIN_ROWS // 2, 0), rows - NA_KROWS)
        pattern = 0 if step == 0 else (2 if step == n_steps - 1 else 1)
        qs = slice(r0 * GRID_W, (r0 + NA_QROWS) * GRID_W)
        kslice = slice(start * GRID_W, (start + NA_KROWS) * GRID_W)
        q = _scaled(q_ref[0, qs, :])
        s_lat = _nt_dot(q, k_ref[0, kslice, :]) + bias_ref[pattern, 0]
        s_ctx = _nt_dot(q, kc)
        o_ref[0, qs, :] = _softmax_pv([s_lat, s_ctx], [v_ref[0, kslice, :], vc], o_ref.dtype)


def na_bias_table(rpb, rows):
    n_steps = rows // NA_QROWS
    n_dr, n_dc = 2 * NA_WIN_ROWS - 1, 2 * NA_WIN_COLS - 1
    col = np.arange(GRID_W)
    col_start = np.clip(col - NA_WIN_COLS // 2, 0, GRID_W - NA_WIN_COLS)
    col_ok = (col[None, :] >= col_start[:, None]) & (col[None, :] < col_start[:, None] + NA_WIN_COLS)
    dc = np.clip(col[None, :] - col[:, None] + NA_WIN_COLS - 1, 0, n_dc - 1)
    pick_dc = dc[:, :, None] == np.arange(n_dc)
    rpb = rpb.astype(F32)
    tables = []
    for step in (0, 1, n_steps - 1):
        r0 = step * NA_QROWS
        start = min(max(r0 - NA_WIN_ROWS // 2, 0), rows - NA_KROWS)
        r = r0 + np.arange(NA_QROWS)
        rs = np.clip(r - NA_WIN_ROWS // 2, 0, rows - NA_WIN_ROWS)
        kr = start + np.arange(NA_KROWS)
        row_ok = (kr[None, :] >= rs[:, None]) & (kr[None, :] < rs[:, None] + NA_WIN_ROWS)
        dr = np.clip(kr[None, :] - r[:, None] + NA_WIN_ROWS - 1, 0, n_dr - 1)
        pick_dr = dr[:, :, None] == np.arange(n_dr)
        by_row = jnp.where(pick_dr[None, :, :, :, None], rpb[:, None, None, :, :], 0.0).sum(axis=3)
        vals = jnp.where(pick_dc[None, None, :, None, :, :], by_row[:, :, None, :, None, :], 0.0).sum(axis=-1)
        ok = row_ok[:, None, :, None] & col_ok[None, :, None, :]
        bias = jnp.where(ok[None], vals, NEG_INF)
        tables.append(bias.reshape(rpb.shape[0], NA_QROWS * GRID_W, NA_KROWS * GRID_W))
    return jnp.stack(tables)


def na_attention(proj, projc, n_heads, q_col0, k_col0, v_col0, kc_col0, vc_col0, bias):
    b, s, _ = proj.shape
    n_ctx = projc.shape[1]
    rows = s // GRID_W
    lat = lambda c0: pl.BlockSpec((1, s, HEAD_DIM), lambda h, bi: (bi, 0, c0 + h))
    ctx = lambda c0: pl.BlockSpec((1, n_ctx, HEAD_DIM), lambda h, bi: (bi, 0, c0 + h))
    return pl.pallas_call(
        functools.partial(_na_kernel, rows=rows),
        out_shape=jax.ShapeDtypeStruct((b, s, n_heads * HEAD_DIM), BF16),
        grid=(n_heads, b),
        in_specs=[lat(q_col0), lat(k_col0), lat(v_col0), ctx(kc_col0), ctx(vc_col0),
                  pl.BlockSpec((3, 1) + bias.shape[2:], lambda h, bi: (0, h, 0, 0))],
        out_specs=pl.BlockSpec((1, s, HEAD_DIM), lambda h, bi: (bi, 0, h)),
        compiler_params=_cparams(("parallel", "parallel")),
        name="na_attention",
    )(proj, proj, proj, projc, projc, bias)


def rope_tables(n_tok):
    t = jnp.arange(n_tok)
    row = (t // GRID_W).astype(F32)
    col = (t % GRID_W).astype(F32)
    n_freq = HEAD_DIM // 4
    inv_freq = ROPE_BASE ** (-jnp.arange(n_freq, dtype=F32) / n_freq)
    ang = jnp.concatenate([row[:, None] * inv_freq, col[:, None] * inv_freq], axis=-1)
    cos, sin = jnp.cos(ang), jnp.sin(ang)
    return jnp.concatenate([cos, cos], axis=-1), jnp.concatenate([-sin, sin], axis=-1)


def _shortconv_kernel(u_ref, w_ref, b_ref, o_ref):
    u = u_ref[0].astype(F32)
    n = u.shape[0]
    row = lax.broadcasted_iota(jnp.int32, u.shape, 0)
    prev = jnp.where(row == 0, 0.0, pltpu.roll(u, 1, axis=0))
    nxt = jnp.where(row == n - 1, 0.0, pltpu.roll(u, n - 1, axis=0))
    o = prev * w_ref[0:1, :] + u * w_ref[1:2, :] + nxt * w_ref[2:3, :] + b_ref[...]
    o_ref[0] = o.astype(o_ref.dtype)


def short_conv(u, w, bias, *, tc=256):
    b, n, c = u.shape
    return pl.pallas_call(
        _shortconv_kernel,
        out_shape=jax.ShapeDtypeStruct((b, n, c), BF16),
        grid=(b, c // tc),
        in_specs=[pl.BlockSpec((1, n, tc), lambda bi, j: (bi, 0, j)),
                  pl.BlockSpec((HY_SHORT, tc), lambda bi, j: (0, j)),
                  pl.BlockSpec((1, tc), lambda bi, j: (0, j))],
        out_specs=pl.BlockSpec((1, n, tc), lambda bi, j: (bi, 0, j)),
        compiler_params=_cparams(("parallel", "parallel")),
        name="short_conv",
    )(u, w, bias.reshape(1, c))


def _filter_mlp_kernel(z_ref, w1_ref, b1_ref, w2_ref, b2_ref, fr_ref, o_ref):
    fr = fr_ref[...]
    a = jnp.sin(fr * (jnp.dot(z_ref[...], w1_ref[...], precision=HIGHEST, preferred_element_type=F32) + b1_ref[...]))
    o_ref[...] = jnp.sin(fr * (jnp.dot(a, w2_ref[...], precision=HIGHEST, preferred_element_type=F32) + b2_ref[...]))


def _split_bf16(x):
    hi = x.astype(BF16)
    return hi, (x - hi.astype(F32)).astype(BF16)


def _dot3(a, w):
    a_hi, a_lo = _split_bf16(a)
    w_hi, w_lo = _split_bf16(w)
    return (jnp.dot(a_hi, w_hi, preferred_element_type=F32) + jnp.dot(a_lo, w_hi, preferred_element_type=F32)
            + jnp.dot(a_hi, w_lo, preferred_element_type=F32))


def _filter_kernel(a_ref, t_ref, w3f_ref, w3b_ref, dl_ref, o_ref, *, n_tok):
    a = a_ref[...]
    f_fwd = _dot3(a, w3f_ref[...])
    f_bwd = _dot3(a, w3b_ref[...])
    window = jnp.exp(-t_ref[...] * dl_ref[...]) + HY_MOD_SHIFT
    lag_row = lax.broadcasted_iota(jnp.int32, f_fwd.shape, 0)
    filt = jnp.where(lag_row >= n_tok, f_fwd, f_bwd) * window
    o_ref[0] = jnp.where(lag_row == 0, 0.0, filt)


def hyena_lag_filters(n_tok, f_w1, f_b1, f_w2, f_b2, f_w3, freq, *, tc=256):
    d = f_w3.shape[1] // (2 * HY_ORDER)
    hidden = f_w1.shape[1]
    bands = (HY_EMB_DIM - 1) // 2
    pos = jnp.abs(jnp.arange(2 * n_tok) - n_tok)
    pos = jnp.minimum(pos, n_tok - 1)
    t = jnp.linspace(0.0, 1.0, n_tok, dtype=F32)[pos][:, None]
    w = (2.0 * math.pi * jnp.arange(n_tok, dtype=F32) / n_tok)[pos][:, None]
    f = jnp.linspace(1e-4, bands - 1, bands, dtype=F32)[None, :]
    z = jnp.concatenate([t, jnp.cos(f * w), -jnp.sin(f * w)], axis=-1)
    z = jnp.pad(z, ((0, 0), (0, LANES - HY_EMB_DIM)))
    w1 = jnp.pad(f_w1.astype(F32), ((0, LANES - HY_EMB_DIM), (0, 0)))
    max_decay = math.log(HY_TARGET) / HY_FAST_DECAY
    min_decay = math.log(HY_TARGET) / HY_SLOW_DECAY
    deltas = jnp.abs(jnp.linspace(min_decay, max_decay, d, dtype=F32)).reshape(1, d)
    nj = d // tc
    n_lag = 2 * n_tok
    one = lambda shape: pl.BlockSpec(shape, lambda i: (0,) * len(shape))
    feats = pl.pallas_call(
        _filter_mlp_kernel,
        out_shape=jax.ShapeDtypeStruct((n_lag, hidden), F32),
        grid=(1,),
        in_specs=[one((n_lag, LANES)), one((LANES, hidden)), one((1, hidden)), one((hidden, hidden)),
                  one((1, hidden)), one((1, hidden))],
        out_specs=one((n_lag, hidden)),
        compiler_params=_cparams(("arbitrary",)),
        name="hyena_filter_mlp",
    )(z, w1, f_b1.reshape(1, hidden).astype(F32), f_w2.astype(F32), f_b2.reshape(1, hidden).astype(F32),
      freq.reshape(1, hidden).astype(F32))
    full = lambda shape: pl.BlockSpec(shape, lambda o, j: (0,) * len(shape))
    return pl.pallas_call(
        functools.partial(_filter_kernel, n_tok=n_tok),
        out_shape=jax.ShapeDtypeStruct((HY_ORDER, n_lag, d), F32),
        grid=(HY_ORDER, nj),
        in_specs=[full((n_lag, hidden)), full((n_lag, 1)),
                  pl.BlockSpec((hidden, tc), lambda o, j: (0, (2 * o) * nj + j)),
                  pl.BlockSpec((hidden, tc), lambda o, j: (0, (2 * o + 1) * nj + j)),
                  pl.BlockSpec((1, tc), lambda o, j: (0, j))],
        out_specs=pl.BlockSpec((1, n_lag, tc), lambda o, j: (o, 0, j)),
        compiler_params=_cparams(("parallel", "parallel")),
        name="hyena_filters",
    )(feats, t, f_w3.astype(F32), f_w3.astype(F32), deltas)


def _dft_angles(f, n, c):
    k = jnp.mod((2 * f + 1) * n, 4 * c).astype(F32)
    return k * (2.0 * math.pi / (4 * c))


def dft_matrices(c):
    f = jnp.arange(c)[:, None]
    ang_u = _dft_angles(f, jnp.arange(c)[None, :], c)
    fwd_u = jnp.concatenate([jnp.cos(ang_u), -jnp.sin(ang_u)], axis=0)
    ang_g = _dft_angles(f, jnp.arange(-c, c)[None, :], c)
    fwd_g = jnp.concatenate([jnp.cos(ang_g), -jnp.sin(ang_g)], axis=0)
    inv = jnp.concatenate([jnp.cos(ang_u).T, -jnp.sin(ang_u).T], axis=1) / c
    return fwd_u.astype(BF16), fwd_g.astype(BF16), inv.astype(BF16)


def _spectra_kernel(fa_ref, fb_ref, ka_ref, kb_ref, o_ref):
    def two_term(f, k):
        hi = k.astype(BF16)
        lo = (k - hi.astype(F32)).astype(BF16)
        return (jnp.dot(f, hi, preferred_element_type=F32) + jnp.dot(f, lo, preferred_element_type=F32))
    o_ref[0, 0] = two_term(fa_ref[...], ka_ref[0]) + two_term(fb_ref[...], kb_ref[0])


def filter_spectra(klag, fwd_g, c, *, tc=256):
    n_ord, two_l, d = klag.shape
    n_delta = two_l // c - 1
    fa, fb = fwd_g[:, :c], fwd_g[:, c:]
    return pl.pallas_call(
        _spectra_kernel,
        out_shape=jax.ShapeDtypeStruct((n_ord, n_delta, 2 * c, d), F32),
        grid=(n_ord, n_delta, d // tc),
        in_specs=[pl.BlockSpec((2 * c, c), lambda o, dl, j: (0, 0)),
                  pl.BlockSpec((2 * c, c), lambda o, dl, j: (0, 0)),
                  pl.BlockSpec((1, c, tc), lambda o, dl, j: (o, dl, j)),
                  pl.BlockSpec((1, c, tc), lambda o, dl, j: (o, dl + 1, j))],
        out_specs=pl.BlockSpec((1, 1, 2 * c, tc), lambda o, dl, j: (o, dl, 0, j)),
        compiler_params=_cparams(("parallel", "parallel", "parallel")),
        name="filter_spectra",
    )(fa, fb, klag, klag)


def _longconv_kernel(u_ref, gate_ref, sk_ref, skip_ref, fu_ref, inv_ref, o_ref, spec_ref, y_ref, *, c, n_ch, order):
    for j in range(n_ch):
        spec_ref[j] = jnp.dot(fu_ref[...], u_ref[0, j * c:(j + 1) * c, :], preferred_element_type=F32)
    skip = skip_ref[order:order + 1, :]
    for i in range(n_ch):
        top = None
        bot = None
        for j in range(n_ch):
            dl = i - j + n_ch - 1
            xr = spec_ref[j, :c, :]
            xi = spec_ref[j, c:, :]
            gr = sk_ref[0, dl, :c, :]
            gi = sk_ref[0, dl, c:, :]
            t = xr * gr - xi * gi
            b = xr * gi + xi * gr
            top = t if top is None else top + t
            bot = b if bot is None else bot + b
        y_ref[:c, :] = top.astype(BF16)
        y_ref[c:, :] = bot.astype(BF16)
        y = jnp.dot(inv_ref[...], y_ref[...], preferred_element_type=F32)
        rows = slice(i * c, (i + 1) * c)
        u = u_ref[0, rows, :].astype(F32)
        o_ref[0, rows, :] = (gate_ref[0, rows, :].astype(F32) * (y + u * skip)).astype(o_ref.dtype)


def long_conv_gate(u_src, u_col0, gate_src, gate_col0, spectra, order, skip, fwd_u, inv, c, *, tc=256):
    b, n, _ = u_src.shape
    d = spectra.shape[3]
    n_ch = n // c
    n_delta = spectra.shape[1]
    nj = d // tc
    return pl.pallas_call(
        functools.partial(_longconv_kernel, c=c, n_ch=n_ch, order=order),
        out_shape=jax.ShapeDtypeStruct((b, n, d), BF16),
        grid=(nj, b),
        in_specs=[pl.BlockSpec((1, n, tc), lambda j, bi: (bi, 0, u_col0 * nj + j)),
                  pl.BlockSpec((1, n, tc), lambda j, bi: (bi, 0, gate_col0 * nj + j)),
                  pl.BlockSpec((1, n_delta, 2 * c, tc), lambda j, bi: (order, 0, 0, j)),
                  pl.BlockSpec((HY_ORDER, tc), lambda j, bi: (0, j)),
                  pl.BlockSpec((2 * c, c), lambda j, bi: (0, 0)),
                  pl.BlockSpec((c, 2 * c), lambda j, bi: (0, 0))],
        out_specs=pl.BlockSpec((1, n, tc), lambda j, bi: (bi, 0, j)),
        scratch_shapes=[pltpu.VMEM((n_ch, 2 * c, tc), F32), pltpu.VMEM((2 * c, tc), BF16)],
        compiler_params=_cparams(("parallel", "parallel")),
        name="long_conv_gate",
    )(u_src, gate_src, spectra, skip.astype(F32), fwd_u, inv)


def _router_kernel(h_ref, w_ref, o_ref):
    logits = jnp.dot(h_ref[...], w_ref[...], precision=HIGHEST, preferred_element_type=F32)
    lane = lax.broadcasted_iota(jnp.int32, logits.shape, 1).astype(F32)
    lg = jnp.where(lane < N_EXPERTS, logits, -jnp.inf)
    m1 = lg.max(axis=-1, keepdims=True)
    e1 = jnp.where(lg == m1, lane, float(LANES)).min(axis=-1, keepdims=True)
    lg2 = jnp.where(lane == e1, -jnp.inf, lg)
    m2 = lg2.max(axis=-1, keepdims=True)
    e2 = jnp.where(lg2 == m2, lane, float(LANES)).min(axis=-1, keepdims=True)
    t = jnp.exp(m2 - m1)
    p1 = 1.0 / (1.0 + t)
    p2 = t / (1.0 + t)
    o_ref[...] = jnp.where(lane == 0, e1, jnp.where(lane == 1, e2, jnp.where(lane == 2, p1,
                           jnp.where(lane == 3, p2, 0.0))))


def moe_route(h, router_w, *, tm=512):
    t, d = h.shape
    w = jnp.pad(router_w.astype(F32), ((0, 0), (0, LANES - N_EXPERTS)))
    return pl.pallas_call(
        _router_kernel,
        out_shape=jax.ShapeDtypeStruct((t, LANES), F32),
        grid=(t // tm,),
        in_specs=[pl.BlockSpec((tm, d), lambda i: (i, 0)), pl.BlockSpec((d, LANES), lambda i: (0, 0))],
        out_specs=pl.BlockSpec((tm, LANES), lambda i: (i, 0)),
        compiler_params=_cparams(("parallel",)),
        name="moe_router",
    )(h, w)


def _row_copy(src_ref, dst_ref, sem, src_row, dst_row):
    return pltpu.make_async_copy(src_ref.at[pl.ds(src_row, 1)], dst_ref.at[pl.ds(dst_row, 1)], sem)


def _load_indices(idx_vmem_ref, idx_smem, isem):
    cp = pltpu.make_async_copy(idx_vmem_ref, idx_smem, isem)
    cp.start()
    cp.wait()


def _issue_rows(idx_smem, src_ref, buf_ref, sem, n):
    def issue(blk, carry):
        for u in range(GATHER_UNROLL):
            r = blk * GATHER_UNROLL + u
            _row_copy(src_ref, buf_ref, sem, idx_smem[0, r], r).start()
        return carry

    lax.fori_loop(0, n // GATHER_UNROLL, issue, 0)


def _wait_rows(src_ref, buf_ref, sem, n):
    def drain(blk, carry):
        for u in range(GATHER_UNROLL):
            _row_copy(src_ref, buf_ref, sem, 0, blk * GATHER_UNROLL + u).wait()
        return carry

    lax.fori_loop(0, n // GATHER_UNROLL, drain, 0)


def _dispatch_kernel(nu_ref, idx_ref, src_ref, o_ref, idx_smem, buf_ref, sem, isem, *, tm):
    used = pl.program_id(0) < nu_ref[0]

    @pl.when(used)
    def _():
        _load_indices(idx_ref.at[0], idx_smem, isem)
        _issue_rows(idx_smem, src_ref, buf_ref, sem, tm)
        _wait_rows(src_ref, buf_ref, sem, tm)
        o_ref[...] = buf_ref[...].astype(o_ref.dtype)

    @pl.when(jnp.logical_not(used))
    def _():
        o_ref[...] = jnp.zeros_like(o_ref)


def moe_dispatch(h, buf_tok, n_used, *, tm=MOE_TILE):
    _, d = h.shape
    n_rows = buf_tok.shape[0]
    idx = buf_tok.reshape(n_rows // tm, 1, tm)
    return pl.pallas_call(
        functools.partial(_dispatch_kernel, tm=tm),
        out_shape=jax.ShapeDtypeStruct((n_rows, d), BF16),
        grid_spec=pltpu.PrefetchScalarGridSpec(
            num_scalar_prefetch=1,
            grid=(n_rows // tm,),
            in_specs=[pl.BlockSpec((1, 1, tm), lambda i, nu: (i, 0, 0)), pl.BlockSpec(memory_space=pl.ANY)],
            out_specs=pl.BlockSpec((tm, d), lambda i, nu: (i, 0)),
            scratch_shapes=[pltpu.SMEM((1, tm), jnp.int32), pltpu.VMEM((tm, d), F32),
                            pltpu.SemaphoreType.DMA, pltpu.SemaphoreType.DMA],
        ),
        compiler_params=_cparams(("arbitrary",)),
        name="moe_dispatch",
    )(n_used, idx, h)


def _combine_kernel(i0_ref, i1_ref, yb_ref, slab_ref, o_ref, idx0_smem, idx1_smem, buf0_ref, buf1_ref,
                    sem0, sem1, isem, *, tm):
    _load_indices(i0_ref.at[0], idx0_smem, isem)
    _issue_rows(idx0_smem, yb_ref, buf0_ref, sem0, tm)
    _load_indices(i1_ref.at[0], idx1_smem, isem)
    _issue_rows(idx1_smem, yb_ref, buf1_ref, sem1, tm)
    _wait_rows(yb_ref, buf0_ref, sem0, tm)
    _wait_rows(yb_ref, buf1_ref, sem1, tm)
    o_ref[...] = buf0_ref[...] * slab_ref[:, 2:3] + buf1_ref[...] * slab_ref[:, 3:4]


def moe_combine(yb, pos0, pos1, slab, *, tm=256):
    t = pos0.shape[0]
    d = yb.shape[1]
    idx_spec = pl.BlockSpec((1, 1, tm), lambda i: (i, 0, 0))
    return pl.pallas_call(
        functools.partial(_combine_kernel, tm=tm),
        out_shape=jax.ShapeDtypeStruct((t, d), F32),
        grid=(t // tm,),
        in_specs=[idx_spec, idx_spec, pl.BlockSpec(memory_space=pl.ANY),
                  pl.BlockSpec((tm, LANES), lambda i: (i, 0))],
        out_specs=pl.BlockSpec((tm, d), lambda i: (i, 0)),
        scratch_shapes=[pltpu.SMEM((1, tm), jnp.int32), pltpu.SMEM((1, tm), jnp.int32),
                        pltpu.VMEM((tm, d), F32), pltpu.VMEM((tm, d), F32),
                        pltpu.SemaphoreType.DMA, pltpu.SemaphoreType.DMA, pltpu.SemaphoreType.DMA],
        compiler_params=_cparams(("arbitrary",)),
        name="moe_combine",
    )(pos0.reshape(t // tm, 1, tm), pos1.reshape(t // tm, 1, tm), yb, slab)


def moe_routing_tables(slab, tile):
    t = slab.shape[0]
    flat_e = slab[:, :2].astype(jnp.int32).reshape(-1)
    n_assign = flat_e.shape[0]
    onehot = (flat_e[:, None] == jnp.arange(N_EXPERTS)[None, :]).astype(jnp.int32)
    csum = jnp.cumsum(onehot, axis=0)
    counts = csum[-1]
    rank = jnp.take_along_axis(csum, flat_e[:, None], axis=1)[:, 0] - 1
    padded = (counts + tile - 1) // tile * tile
    pad_end = jnp.cumsum(padded)
    pad_start = pad_end - padded
    dest = (pad_start[flat_e] + rank).astype(jnp.int32)
    n_blocks = (n_assign + N_EXPERTS * (tile - 1) + tile - 1) // tile
    buf_tok = jnp.zeros((n_blocks * tile,), jnp.int32).at[dest].set(jnp.arange(n_assign, dtype=jnp.int32) // 2)
    block_e = jnp.minimum(jnp.searchsorted(pad_end, jnp.arange(n_blocks) * tile, side='right'),
                          N_EXPERTS - 1).astype(jnp.int32)
    dest = dest.reshape(t, 2)
    n_used = (pad_end[-1:] // tile).astype(jnp.int32)
    return buf_tok, block_e, n_used, dest[:, 0], dest[:, 1]


def moe_swiglu(h, router_w, w1, w3, w2, layer):
    slab = moe_route(h, router_w)
    buf_tok, block_e, n_used, pos0, pos1 = moe_routing_tables(slab, MOE_TILE)
    block_e = block_e + layer * N_EXPERTS
    xb = moe_dispatch(h, buf_tok, n_used)
    mid = swiglu_up(xb, w1, w3, block_e, tm=MOE_TILE, tn=512, n_used=n_used)
    yb = gmm(mid, w2, block_e, tm=MOE_TILE, tn=512, out_dtype=F32, n_used=n_used)
    return moe_combine(yb, pos0, pos1, slab)


def _row_tile(k):
    return 1024 if k <= 2048 else 512


def _dense_ids(n_rows, tm, idx):
    return jnp.full((n_rows // tm,), idx, jnp.int32)


def _dense(a3, w, idx, *, bias=None, out_dtype=BF16, tn=512, col_off=0, n_out=None):
    b, s, k = a3.shape
    m = b * s
    tm = min(_row_tile(k), m)
    out = gmm(a3.reshape(m, k), w, _dense_ids(m, tm, idx), bias, tm=tm, tn=tn, out_dtype=out_dtype,
              col_off=col_off, n_out=n_out)
    return out.reshape(b, s, -1)


def _dense_swiglu(h3, w1, w3, w2, idx):
    b, s, k = h3.shape
    m = b * s
    tm = min(1024, m)
    mid = swiglu_up(h3.reshape(m, k), w1, w3, _dense_ids(m, tm, idx), tm=tm, tn=512)
    tm2 = min(_row_tile(mid.shape[1]), m)
    return gmm(mid, w2, _dense_ids(m, tm2, idx), tm=tm2, tn=512, out_dtype=BF16).reshape(b, s, -1)


def _attention_layer(h, hc, i, w_in, w_out, q_g, k_g, bias, cos2, sin2, ctx_out):
    d = h.shape[-1]
    n_na = d // (2 * HEAD_DIM)
    n_q = n_na
    n_kv = max(1, n_q // 4)
    group = n_q // n_kv
    c_bq, c_ak, c_av = n_na, n_na + n_q, 2 * n_na + n_q
    c_bk, c_bv = 3 * n_na + n_q, 3 * n_na + n_q + n_kv
    q_cols = (n_na + n_q) * HEAD_DIM
    proj = _dense(h, w_in, i)
    if ctx_out:
        projc = _dense(hc, w_in, i)
        shift = 0
    else:
        projc = _dense(hc, w_in, i, col_off=q_cols // 512, n_out=w_in.shape[2] - q_cols)
        shift = n_na + n_q
    qb = head_prep(proj, c_bq, n_q, q_g, cos2, sin2, rope=True)
    kb = head_prep(proj, c_bk, n_kv, k_g, cos2, sin2, rope=True)
    kbc = head_prep(projc, c_bk - shift, n_kv, k_g, cos2, sin2, rope=False)
    o_a = na_attention(proj, projc, n_na, 0, c_ak, c_av, c_ak - shift, c_av - shift, bias)
    o_b = full_attention(qb, 0, n_kv, group, [(kbc, 0), (kb, 0)], [(projc, c_bv - shift), (proj, c_bv)])
    y = _dense(jnp.concatenate([o_a, o_b], axis=-1), w_out, i)
    if not ctx_out:
        return y, None
    qbc = head_prep(projc, c_bq, n_q, q_g, cos2, sin2, rope=False)
    oc_a = full_attention(projc, 0, n_na, 1, [(projc, c_ak)], [(projc, c_av)])
    oc_b = full_attention(qbc, 0, n_kv, group, [(kbc, 0)], [(projc, c_bv)])
    yc = _dense(jnp.concatenate([oc_a, oc_b], axis=-1), w_out, i)
    return y, yc


def _hyena_layer(h, i, p):
    (w_in, b_in, conv_w, conv_b, f_w1, f_b1, f_w2, f_b2, f_w3, freq, skip, w_out, b_out) = p
    n_tok = h.shape[1]
    c = min(HY_CHUNK, n_tok)
    u0 = _dense(h, w_in, i, bias=b_in[i].reshape(1, -1))
    u = short_conv(u0, conv_w[i], conv_b[i])
    fwd_u, fwd_g, inv = dft_matrices(c)
    klag = hyena_lag_filters(n_tok, f_w1[i], f_b1[i], f_w2[i], f_b2[i], f_w3[i], freq[i])
    spectra = filter_spectra(klag, fwd_g, c)
    z = long_conv_gate(u, 0, u, 1, spectra, 0, skip[i], fwd_u, inv, c)
    z = long_conv_gate(z, 0, u, 2, spectra, 1, skip[i], fwd_u, inv, c)
    return _dense(z, w_out, i, bias=b_out[i].reshape(1, -1))


def kernel(x, c, ctx, c_ctx, mod_w, mod_b, norm_g, att_w_in, att_w_out, att_q_norm_g, att_k_norm_g, na_rpb,
           ffn_w1, ffn_w3, ffn_w2, hy_w_in, hy_b_in, hy_conv_w, hy_conv_b, hy_f_w1, hy_f_b1, hy_f_w2, hy_f_b2,
           hy_f_w3, hy_freq, hy_skip, hy_w_out, hy_b_out, moe_router, moe_w1, moe_w3, moe_w2):
    depth = mod_w.shape[0]
    b, s, d = x.shape
    n_ctx = ctx.shape[1]
    d_ff = moe_w1.shape[-1]
    cos2, sin2 = rope_tables(s)
    moe_w1f = moe_w1.reshape(-1, d, d_ff)
    moe_w3f = moe_w3.reshape(-1, d, d_ff)
    moe_w2f = moe_w2.reshape(-1, d_ff, d)
    hy = (hy_w_in, hy_b_in, hy_conv_w, hy_conv_b, hy_f_w1, hy_f_b1, hy_f_w2, hy_f_b2, hy_f_w3, hy_freq,
          hy_skip, hy_w_out, hy_b_out)

    n_mod_rows = 16
    cond = jnp.concatenate([c, c_ctx[None, :], jnp.zeros((n_mod_rows - b - 1, d), F32)], axis=0)

    lat_mods, ctx_mods = [], []
    for l in range(depth):
        mod = gmm(cond, mod_w, _dense_ids(n_mod_rows, n_mod_rows, l), mod_b[l].reshape(1, -1),
                  tm=n_mod_rows, tn=512, out_dtype=F32, silu_in=True)
        m_lat = mod[:b].reshape(b, 6, 1, d)
        lat_mods.append([m_lat[:, k] for k in range(6)])
        m_ctx = jnp.broadcast_to(mod[b].reshape(1, 6, 1, d), (b, 6, 1, d))
        ctx_mods.append([m_ctx[:, k] for k in range(6)])

    xc = ctx
    h = modulate(x, norm_g[0, 0], lat_mods[0][0], lat_mods[0][1], out_dtype=BF16)
    for l in range(depth):
        even = l % 2 == 0
        i = l // 2
        ctx_out = any(m % 2 == 0 for m in range(l + 1, depth))
        ctx_in = even or ctx_out
        sh1, sc1, g1, sh2, sc2, g2 = lat_mods[l]
        csh1, csc1, cg1, csh2, csc2, cg2 = ctx_mods[l]

        hc = modulate(xc, norm_g[l, 0], csh1, csc1, out_dtype=BF16) if ctx_in else None
        if even:
            bias = na_bias_table(na_rpb[i], s // GRID_W)
            y, yc = _attention_layer(h, hc, i, att_w_in, att_w_out, att_q_norm_g[i], att_k_norm_g[i], bias,
                                     cos2, sin2, ctx_out)
        else:
            y = _hyena_layer(h, i, hy)
            yc = _hyena_layer(hc, i, hy) if ctx_out else None
        if even:
            x, h = residual_modulate(x, y, norm_g[l, 1], g1, norm_g[l, 2], sh2, sc2, out_dtype=BF16)
            y = _dense_swiglu(h, ffn_w1, ffn_w3, ffn_w2, i)
        else:
            x, h = residual_modulate(x, y, norm_g[l, 1], g1, norm_g[l, 2], sh2, sc2, out_dtype=F32)
            y = moe_swiglu(h.reshape(b * s, d), moe_router[i], moe_w1f, moe_w3f, moe_w2f, i).reshape(b, s, d)
        if l + 1 < depth:
            x, h = residual_modulate(x, y, norm_g[l, 3], g2, norm_g[l + 1, 0], lat_mods[l + 1][0],
                                     lat_mods[l + 1][1], out_dtype=BF16)
        else:
            x = residual(x, y, norm_g[l, 3], g2)
        if ctx_out:
            xc = residual(xc, yc, norm_g[l, 1], cg1)
            if even:
                hc = modulate(xc, norm_g[l, 2], csh2, csc2, out_dtype=BF16)
                yc = _dense_swiglu(hc, ffn_w1, ffn_w3, ffn_w2, i)
            else:
                hc = modulate(xc, norm_g[l, 2], csh2, csc2, out_dtype=F32)
                yc = moe_swiglu(hc.reshape(b * n_ctx, d), moe_router[i], moe_w1f, moe_w3f, moe_w2f,
                                i).reshape(b, n_ctx, d)
            xc = residual(xc, yc, norm_g[l, 3], cg2)
    return x
```

```python
import functools
import math

import jax
import jax.numpy as jnp
import numpy as np
from jax import lax
from jax.experimental import pallas as pl
from jax.experimental.pallas import tpu as pltpu

F32 = jnp.float32
BF16 = jnp.bfloat16
HIGHEST = lax.Precision.HIGHEST

GRID_W = 64
HEAD_DIM = 128
NA_WIN_ROWS = 8
NA_WIN_COLS = 16
ROPE_BASE = 10000.0
HY_ORDER = 2
HY_SHORT = 3
HY_EMB_DIM = 33
HY_TARGET = 1e-2
HY_FAST_DECAY = 0.3
HY_SLOW_DECAY = 1.5
HY_MOD_SHIFT = 0.05
N_EXPERTS = 8
NORM_EPS = 1e-6
NEG_INF = -1e30

LANES = 128
V7X_VMEM_BYTES = 64 * 1024 * 1024
VMEM_LIMIT = 56 * 1024 * 1024

ROW_TILE = 512
MOE_TILE = 512
GATHER_UNROLL = 8
HY_CHUNK = 512
NA_QROWS = 4
NA_KROWS = 12


def _cparams(sem):
    return pltpu.CompilerParams(dimension_semantics=sem, vmem_limit_bytes=VMEM_LIMIT)


def _weights_changed(be_ref, i):
    prev = be_ref[jnp.maximum(i - 1, 0)]
    return (i == 0) | (be_ref[i] != prev)


def _gmm_kernel(be_ref, nu_ref, a_ref, w_ref, b_ref, o_ref, wbf_ref, *, silu_in):
    i = pl.program_id(1)
    used = i < nu_ref[0]

    @pl.when(_weights_changed(be_ref, i))
    def _():
        wbf_ref[...] = w_ref[0].astype(BF16)

    @pl.when(used)
    def _():
        a = a_ref[...]
        if silu_in:
            a = a.astype(F32)
            a = a * jax.nn.sigmoid(a)
        acc = jnp.dot(a.astype(BF16), wbf_ref[...], preferred_element_type=F32)
        o_ref[...] = (acc + b_ref[...]).astype(o_ref.dtype)

    @pl.when(jnp.logical_not(used))
    def _():
        o_ref[...] = jnp.zeros_like(o_ref)


def _all_used(block_e):
    return jnp.full((1,), block_e.shape[0], jnp.int32)


def gmm(a, w, block_e, bias=None, *, tm, tn, out_dtype, col_off=0, n_out=None, silu_in=False, n_used=None):
    m, k = a.shape
    n_out = w.shape[2] if n_out is None else n_out
    if bias is None:
        bias = jnp.zeros((1, w.shape[2]), F32)
    n_used = _all_used(block_e) if n_used is None else n_used
    grid = (n_out // tn, m // tm)
    return pl.pallas_call(
        functools.partial(_gmm_kernel, silu_in=silu_in),
        out_shape=jax.ShapeDtypeStruct((m, n_out), out_dtype),
        grid_spec=pltpu.PrefetchScalarGridSpec(
            num_scalar_prefetch=2,
            grid=grid,
            in_specs=[
                pl.BlockSpec((tm, k), lambda j, i, be, nu: (jnp.minimum(i, nu[0] - 1), 0)),
                pl.BlockSpec((1, k, tn), lambda j, i, be, nu: (be[i], 0, j + col_off)),
                pl.BlockSpec((1, tn), lambda j, i, be, nu: (0, j + col_off)),
            ],
            out_specs=pl.BlockSpec((tm, tn), lambda j, i, be, nu: (i, j)),
            scratch_shapes=[pltpu.VMEM((k, tn), BF16)],
        ),
        compiler_params=_cparams(("arbitrary", "arbitrary")),
        name="gmm",
    )(block_e, n_used, a, w, bias)


def _up_kernel(be_ref, nu_ref, a_ref, w1_ref, w3_ref, o_ref, w1bf_ref, w3bf_ref):
    i = pl.program_id(1)
    used = i < nu_ref[0]

    @pl.when(_weights_changed(be_ref, i))
    def _():
        w1bf_ref[...] = w1_ref[0].astype(BF16)
        w3bf_ref[...] = w3_ref[0].astype(BF16)

    @pl.when(used)
    def _():
        a = a_ref[...]
        g = jnp.dot(a, w1bf_ref[...], preferred_element_type=F32)
        u = jnp.dot(a, w3bf_ref[...], preferred_element_type=F32)
        o_ref[...] = (g * jax.nn.sigmoid(g) * u).astype(o_ref.dtype)

    @pl.when(jnp.logical_not(used))
    def _():
        o_ref[...] = jnp.zeros_like(o_ref)


def swiglu_up(a, w1, w3, block_e, *, tm, tn, n_used=None):
    m, k = a.shape
    n = w1.shape[2]
    n_used = _all_used(block_e) if n_used is None else n_used
    grid = (n // tn, m // tm)
    return pl.pallas_call(
        _up_kernel,
        out_shape=jax.ShapeDtypeStruct((m, n), BF16),
        grid_spec=pltpu.PrefetchScalarGridSpec(
            num_scalar_prefetch=2,
            grid=grid,
            in_specs=[
                pl.BlockSpec((tm, k), lambda j, i, be, nu: (jnp.minimum(i, nu[0] - 1), 0)),
                pl.BlockSpec((1, k, tn), lambda j, i, be, nu: (be[i], 0, j)),
                pl.BlockSpec((1, k, tn), lambda j, i, be, nu: (be[i], 0, j)),
            ],
            out_specs=pl.BlockSpec((tm, tn), lambda j, i, be, nu: (i, j)),
            scratch_shapes=[pltpu.VMEM((k, tn), BF16), pltpu.VMEM((k, tn), BF16)],
        ),
        compiler_params=_cparams(("arbitrary", "arbitrary")),
        name="swiglu_up",
    )(block_e, n_used, a, w1, w3)


def _rms(x, g):
    ms = jnp.mean(x * x, axis=-1, keepdims=True)
    return x * lax.rsqrt(ms + NORM_EPS) * g


def _mod_kernel(x_ref, g_ref, sh_ref, sc_ref, o_ref):
    y = _rms(x_ref[0], g_ref[...])
    o_ref[0] = (y * (1.0 + sc_ref[0]) + sh_ref[0]).astype(o_ref.dtype)


def modulate(x, g, shift, scale, *, out_dtype, tr=ROW_TILE):
    b, s, d = x.shape
    tr = min(tr, s)
    row = pl.BlockSpec((1, tr, d), lambda bi, si: (bi, si, 0))
    vec = pl.BlockSpec((1, 1, d), lambda bi, si: (bi, 0, 0))
    return pl.pallas_call(
        _mod_kernel,
        out_shape=jax.ShapeDtypeStruct((b, s, d), out_dtype),
        grid=(b, s // tr),
        in_specs=[row, pl.BlockSpec((1, d), lambda bi, si: (0, 0)), vec, vec],
        out_specs=row,
        compiler_params=_cparams(("parallel", "parallel")),
        name="modulate",
    )(x, g.reshape(1, d), shift, scale)


def _resid_kernel(x_ref, y_ref, g_ref, gate_ref, o_ref):
    y = _rms(y_ref[0].astype(F32), g_ref[...])
    o_ref[0] = x_ref[0] + gate_ref[0] * y


def residual(x, y, g, gate, *, tr=ROW_TILE, y_off=0):
    b, s, d = x.shape
    tr = min(tr, s)
    row = pl.BlockSpec((1, tr, d), lambda bi, si: (bi, si, 0))
    y_row = pl.BlockSpec((1, tr, d), lambda bi, si: (bi + y_off, si, 0))
    vec = pl.BlockSpec((1, 1, d), lambda bi, si: (bi, 0, 0))
    return pl.pallas_call(
        _resid_kernel,
        out_shape=jax.ShapeDtypeStruct((b, s, d), F32),
        grid=(b, s // tr),
        in_specs=[row, y_row, pl.BlockSpec((1, d), lambda bi, si: (0, 0)), vec],
        out_specs=row,
        compiler_params=_cparams(("parallel", "parallel")),
        name="residual",
    )(x, y, g.reshape(1, d), gate)


def _resid_mod_kernel(x_ref, y_ref, gy_ref, gate_ref, gh_ref, sh_ref, sc_ref, xo_ref, ho_ref):
    x = x_ref[0] + gate_ref[0] * _rms(y_ref[0].astype(F32), gy_ref[...])
    xo_ref[0] = x
    ho_ref[0] = (_rms(x, gh_ref[...]) * (1.0 + sc_ref[0]) + sh_ref[0]).astype(ho_ref.dtype)


def residual_modulate(x, y, g_y, gate, g_h, shift, scale, *, out_dtype, tr=ROW_TILE, y_off=0):
    b, s, d = x.shape
    tr = min(tr, s)
    row = pl.BlockSpec((1, tr, d), lambda bi, si: (bi, si, 0))
    y_row = pl.BlockSpec((1, tr, d), lambda bi, si: (bi + y_off, si, 0))
    vec = pl.BlockSpec((1, 1, d), lambda bi, si: (bi, 0, 0))
    gain = pl.BlockSpec((1, d), lambda bi, si: (0, 0))
    return pl.pallas_call(
        _resid_mod_kernel,
        out_shape=(jax.ShapeDtypeStruct((b, s, d), F32), jax.ShapeDtypeStruct((b, s, d), out_dtype)),
        grid=(b, s // tr),
        in_specs=[row, y_row, gain, vec, gain, vec, vec],
        out_specs=(row, row),
        compiler_params=_cparams(("parallel", "parallel")),
        name="residual_modulate",
    )(x, y, g_y.reshape(1, d), gate, g_h.reshape(1, d), shift, scale)


def _prep_kernel(x_ref, g_ref, cos_ref, sin_ref, o_ref, *, rope, n_heads):
    for h in range(n_heads):
        cols = slice(h * HEAD_DIM, (h + 1) * HEAD_DIM)
        y = _rms(x_ref[0, :, cols].astype(F32), g_ref[...])
        if rope:
            rot = pltpu.roll(y, HEAD_DIM // 2, axis=1)
            y = y * cos_ref[...] + rot * sin_ref[...]
        o_ref[0, :, cols] = y.astype(o_ref.dtype)


def head_prep(src, col0, n_heads, g, cos2, sin2, *, rope):
    b, s, _ = src.shape
    ts = min(s, 512)
    width = n_heads * HEAD_DIM
    assert col0 % n_heads == 0
    return pl.pallas_call(
        functools.partial(_prep_kernel, rope=rope, n_heads=n_heads),
        out_shape=jax.ShapeDtypeStruct((b, s, width), BF16),
        grid=(b, s // ts),
        in_specs=[
            pl.BlockSpec((1, ts, width), lambda bi, si: (bi, si, col0 // n_heads)),
            pl.BlockSpec((1, HEAD_DIM), lambda bi, si: (0, 0)),
            pl.BlockSpec((ts, HEAD_DIM), lambda bi, si: (si, 0)),
            pl.BlockSpec((ts, HEAD_DIM), lambda bi, si: (si, 0)),
        ],
        out_specs=pl.BlockSpec((1, ts, width), lambda bi, si: (bi, si, 0)),
        compiler_params=_cparams(("parallel", "parallel")),
        name="head_prep",
    )(src, g.reshape(1, HEAD_DIM), cos2[:s], sin2[:s])


def _nt_dot(q, k):
    return lax.dot_general(q, k, (((1,), (1,)), ((), ())), preferred_element_type=F32)


def _softmax_pv(scores, values, out_dtype):
    m = scores[0].max(axis=-1, keepdims=True)
    for s in scores[1:]:
        m = jnp.maximum(m, s.max(axis=-1, keepdims=True))
    ps = [jnp.exp(s - m) for s in scores]
    l = ps[0].sum(axis=-1, keepdims=True)
    for p in ps[1:]:
        l = l + p.sum(axis=-1, keepdims=True)
    o = jnp.dot(ps[0].astype(BF16), values[0], preferred_element_type=F32)
    for p, v in zip(ps[1:], values[1:]):
        o = o + jnp.dot(p.astype(BF16), v, preferred_element_type=F32)
    return (o / l).astype(out_dtype)


def _scaled(q):
    return (q.astype(F32) * (HEAD_DIM ** -0.5)).astype(BF16)


def _attn_kernel(*refs, group, n_parts):
    q_ref = refs[0]
    k_refs = refs[1:1 + n_parts]
    v_refs = refs[1 + n_parts:1 + 2 * n_parts]
    o_ref = refs[1 + 2 * n_parts]
    ks = [r[0] for r in k_refs]
    vs = [r[0] for r in v_refs]
    for g in range(group):
        cols = slice(g * HEAD_DIM, (g + 1) * HEAD_DIM)
        q = _scaled(q_ref[0, :, cols])
        scores = [_nt_dot(q, k) for k in ks]
        o_ref[0, :, cols] = _softmax_pv(scores, vs, o_ref.dtype)


def full_attention(q_src, q_col0, n_kv, group, k_parts, v_parts, *, tq=512):
    b, sq, _ = q_src.shape
    tq = min(tq, sq)
    gw = group * HEAD_DIM
    assert (q_col0 * HEAD_DIM) % gw == 0
    q_blk0 = q_col0 * HEAD_DIM // gw
    in_specs = [pl.BlockSpec((1, tq, gw), lambda bi, h, qi: (bi, qi, q_blk0 + h))]
    args = [q_src]
    for arr, c0 in list(k_parts) + list(v_parts):
        in_specs.append(pl.BlockSpec((1, arr.shape[1], HEAD_DIM),
                                     functools.partial(lambda bi, h, qi, c0: (bi, 0, c0 + h), c0=c0)))
        args.append(arr)
    return pl.pallas_call(
        functools.partial(_attn_kernel, group=group, n_parts=len(k_parts)),
        out_shape=jax.ShapeDtypeStruct((b, sq, n_kv * gw), BF16),
        grid=(b, n_kv, sq // tq),
        in_specs=in_specs,
        out_specs=pl.BlockSpec((1, tq, gw), lambda bi, h, qi: (bi, qi, h)),
        compiler_params=_cparams(("parallel", "parallel", "parallel")),
        name="full_attention",
    )(*args)


def _na_kernel(q_ref, k_ref, v_ref, kc_ref, vc_ref, bias_ref, o_ref, *, rows):
    n_steps = rows // NA_QROWS
    kc = kc_ref[0]
    vc = vc_ref[0]
    for step in range(n_steps):
        r0 = step * NA_QROWS
        start = min(max(r0 - NA_WIN_ROWS // 2, 0), rows - NA_KROWS)
        pattern = 0 if step == 0 else (2 if step == n_steps - 1 else 1)
        qs = slice(r0 * GRID_W, (r0 + NA_QROWS) * GRID_W)
        kslice = slice(start * GRID_W, (start + NA_KROWS) * GRID_W)
        q = _scaled(q_ref[0, qs, :])
        s_lat = _nt_dot(q, k_ref[0, kslice, :]) + bias_ref[pattern, 0]
        s_ctx = _nt_dot(q, kc)
        o_ref[0, qs, :] = _softmax_pv([s_lat, s_ctx], [v_ref[0, kslice, :], vc], o_ref.dtype)


def na_bias_table(rpb, rows):
    n_steps = rows // NA_QROWS
    n_dr, n_dc = 2 * NA_WIN_ROWS - 1, 2 * NA_WIN_COLS - 1
    col = np.arange(GRID_W)
    col_start = np.clip(col - NA_WIN_COLS // 2, 0, GRID_W - NA_WIN_COLS)
    col_ok = (col[None, :] >= col_start[:, None]) & (col[None, :] < col_start[:, None] + NA_WIN_COLS)
    dc = np.clip(col[None, :] - col[:, None] + NA_WIN_COLS - 1, 0, n_dc - 1)
    pick_dc = dc[:, :, None] == np.arange(n_dc)
    rpb = rpb.astype(F32)
    tables = []
    for step in (0, 1, n_steps - 1):
        r0 = step * NA_QROWS
        start = min(max(r0 - NA_WIN_ROWS // 2, 0), rows - NA_KROWS)
        r = r0 + np.arange(NA_QROWS)
        rs = np.clip(r - NA_WIN_ROWS // 2, 0, rows - NA_WIN_ROWS)
        kr = start + np.arange(NA_KROWS)
        row_ok = (kr[None, :] >= rs[:, None]) & (kr[None, :] < rs[:, None] + NA_WIN_ROWS)
        dr = np.clip(kr[None, :] - r[:, None] + NA_WIN_ROWS - 1, 0, n_dr - 1)
        pick_dr = dr[:, :, None] == np.arange(n_dr)
        by_row = jnp.einsum('hde,ijd->hije', rpb, pick_dr.astype(F32), precision=HIGHEST)
        vals = jnp.einsum('hije,qke->hiqjk', by_row, pick_dc.astype(F32), precision=HIGHEST)
        ok = row_ok[:, None, :, None] & col_ok[None, :, None, :]
        bias = jnp.where(ok[None], vals, NEG_INF)
        tables.append(bias.reshape(rpb.shape[0], NA_QROWS * GRID_W, NA_KROWS * GRID_W))
    return jnp.stack(tables)


def na_attention(proj, projc, n_heads, q_col0, k_col0, v_col0, kc_col0, vc_col0, bias):
    b, s, _ = proj.shape
    n_ctx = projc.shape[1]
    rows = s // GRID_W
    lat = lambda c0: pl.BlockSpec((1, s, HEAD_DIM), lambda h, bi: (bi, 0, c0 + h))
    ctx = lambda c0: pl.BlockSpec((1, n_ctx, HEAD_DIM), lambda h, bi: (bi, 0, c0 + h))
    return pl.pallas_call(
        functools.partial(_na_kernel, rows=rows),
        out_shape=jax.ShapeDtypeStruct((b, s, n_heads * HEAD_DIM), BF16),
        grid=(n_heads, b),
        in_specs=[lat(q_col0), lat(k_col0), lat(v_col0), ctx(kc_col0), ctx(vc_col0),
                  pl.BlockSpec((3, 1) + bias.shape[2:], lambda h, bi: (0, h, 0, 0))],
        out_specs=pl.BlockSpec((1, s, HEAD_DIM), lambda h, bi: (bi, 0, h)),
        compiler_params=_cparams(("parallel", "parallel")),
        name="na_attention",
    )(proj, proj, proj, projc, projc, bias)


def rope_tables(n_tok):
    t = jnp.arange(n_tok)
    row = (t // GRID_W).astype(F32)
    col = (t % GRID_W).astype(F32)
    n_freq = HEAD_DIM // 4
    inv_freq = ROPE_BASE ** (-jnp.arange(n_freq, dtype=F32) / n_freq)
    ang = jnp.concatenate([row[:, None] * inv_freq, col[:, None] * inv_freq], axis=-1)
    cos, sin = jnp.cos(ang), jnp.sin(ang)
    return jnp.concatenate([cos, cos], axis=-1), jnp.concatenate([-sin, sin], axis=-1)


def _shortconv_kernel(u_ref, w_ref, b_ref, o_ref):
    u = u_ref[0].astype(F32)
    n = u.shape[0]
    row = lax.broadcasted_iota(jnp.int32, u.shape, 0)
    prev = jnp.where(row == 0, 0.0, pltpu.roll(u, 1, axis=0))
    nxt = jnp.where(row == n - 1, 0.0, pltpu.roll(u, n - 1, axis=0))
    o = prev * w_ref[0:1, :] + u * w_ref[1:2, :] + nxt * w_ref[2:3, :] + b_ref[...]
    o_ref[0] = o.astype(o_ref.dtype)


def short_conv(u, w, bias, *, tc=256):
    b, n, c = u.shape
    return pl.pallas_call(
        _shortconv_kernel,
        out_shape=jax.ShapeDtypeStruct((b, n, c), BF16),
        grid=(b, c // tc),
        in_specs=[pl.BlockSpec((1, n, tc), lambda bi, j: (bi, 0, j)),
                  pl.BlockSpec((HY_SHORT, tc), lambda bi, j: (0, j)),
                  pl.BlockSpec((1, tc), lambda bi, j: (0, j))],
        out_specs=pl.BlockSpec((1, n, tc), lambda bi, j: (bi, 0, j)),
        compiler_params=_cparams(("parallel", "parallel")),
        name="short_conv",
    )(u, w, bias.reshape(1, c))


def _filter_mlp_kernel(z_ref, w1_ref, b1_ref, w2_ref, b2_ref, fr_ref, o_ref):
    fr = fr_ref[...]
    a = jnp.sin(fr * (jnp.dot(z_ref[...], w1_ref[...], precision=HIGHEST, preferred_element_type=F32) + b1_ref[...]))
    o_ref[...] = jnp.sin(fr * (jnp.dot(a, w2_ref[...], precision=HIGHEST, preferred_element_type=F32) + b2_ref[...]))


def _split_bf16(x):
    hi = x.astype(BF16)
    return hi, (x - hi.astype(F32)).astype(BF16)


def _dot3(a, w):
    a_hi, a_lo = _split_bf16(a)
    w_hi, w_lo = _split_bf16(w)
    return (jnp.dot(a_hi, w_hi, preferred_element_type=F32) + jnp.dot(a_lo, w_hi, preferred_element_type=F32)
            + jnp.dot(a_hi, w_lo, preferred_element_type=F32))


def _filter_kernel(a_ref, t_ref, w3f_ref, w3b_ref, dl_ref, o_ref, *, n_tok):
    a = a_ref[...]
    f_fwd = _dot3(a, w3f_ref[...])
    f_bwd = _dot3(a, w3b_ref[...])
    window = jnp.exp(-t_ref[...] * dl_ref[...]) + HY_MOD_SHIFT
    lag_row = lax.broadcasted_iota(jnp.int32, f_fwd.shape, 0)
    filt = jnp.where(lag_row >= n_tok, f_fwd, f_bwd) * window
    o_ref[0] = jnp.where(lag_row == 0, 0.0, filt)


def hyena_lag_filters(n_tok, f_w1, f_b1, f_w2, f_b2, f_w3, freq, *, tc=256):
    d = f_w3.shape[1] // (2 * HY_ORDER)
    hidden = f_w1.shape[1]
    bands = (HY_EMB_DIM - 1) // 2
    pos = jnp.abs(jnp.arange(2 * n_tok) - n_tok)
    pos = jnp.minimum(pos, n_tok - 1)
    t = jnp.linspace(0.0, 1.0, n_tok, dtype=F32)[pos][:, None]
    w = (2.0 * math.pi * jnp.arange(n_tok, dtype=F32) / n_tok)[pos][:, None]
    f = jnp.linspace(1e-4, bands - 1, bands, dtype=F32)[None, :]
    z = jnp.concatenate([t, jnp.cos(f * w), -jnp.sin(f * w)], axis=-1)
    z = jnp.pad(z, ((0, 0), (0, LANES - HY_EMB_DIM)))
    w1 = jnp.pad(f_w1.astype(F32), ((0, LANES - HY_EMB_DIM), (0, 0)))
    max_decay = math.log(HY_TARGET) / HY_FAST_DECAY
    min_decay = math.log(HY_TARGET) / HY_SLOW_DECAY
    deltas = jnp.abs(jnp.linspace(min_decay, max_decay, d, dtype=F32)).reshape(1, d)
    nj = d // tc
    n_lag = 2 * n_tok
    one = lambda shape: pl.BlockSpec(shape, lambda i: (0,) * len(shape))
    feats = pl.pallas_call(
        _filter_mlp_kernel,
        out_shape=jax.ShapeDtypeStruct((n_lag, hidden), F32),
        grid=(1,),
        in_specs=[one((n_lag, LANES)), one((LANES, hidden)), one((1, hidden)), one((hidden, hidden)),
                  one((1, hidden)), one((1, hidden))],
        out_specs=one((n_lag, hidden)),
        compiler_params=_cparams(("arbitrary",)),
        name="hyena_filter_mlp",
    )(z, w1, f_b1.reshape(1, hidden).astype(F32), f_w2.astype(F32), f_b2.reshape(1, hidden).astype(F32),
      freq.reshape(1, hidden).astype(F32))
    full = lambda shape: pl.BlockSpec(shape, lambda o, j: (0,) * len(shape))
    return pl.pallas_call(
        functools.partial(_filter_kernel, n_tok=n_tok),
        out_shape=jax.ShapeDtypeStruct((HY_ORDER, n_lag, d), F32),
        grid=(HY_ORDER, nj),
        in_specs=[full((n_lag, hidden)), full((n_lag, 1)),
                  pl.BlockSpec((hidden, tc), lambda o, j: (0, (2 * o) * nj + j)),
                  pl.BlockSpec((hidden, tc), lambda o, j: (0, (2 * o + 1) * nj + j)),
                  pl.BlockSpec((1, tc), lambda o, j: (0, j))],
        out_specs=pl.BlockSpec((1, n_lag, tc), lambda o, j: (o, 0, j)),
        compiler_params=_cparams(("parallel", "parallel")),
        name="hyena_filters",
    )(feats, t, f_w3.astype(F32), f_w3.astype(F32), deltas)


def _dft_angles(f, n, c):
    k = jnp.mod((2 * f + 1) * n, 4 * c).astype(F32)
    return k * (2.0 * math.pi / (4 * c))


def dft_matrices(c):
    f = jnp.arange(c)[:, None]
    ang_u = _dft_angles(f, jnp.arange(c)[None, :], c)
    fwd_u = jnp.concatenate([jnp.cos(ang_u), -jnp.sin(ang_u)], axis=0)
    ang_g = _dft_angles(f, jnp.arange(-c, c)[None, :], c)
    fwd_g = jnp.concatenate([jnp.cos(ang_g), -jnp.sin(ang_g)], axis=0)
    inv = jnp.concatenate([jnp.cos(ang_u).T, -jnp.sin(ang_u).T], axis=1) / c
    return fwd_u.astype(BF16), fwd_g.astype(BF16), inv.astype(BF16)


def _spectra_kernel(fa_ref, fb_ref, ka_ref, kb_ref, o_ref):
    def two_term(f, k):
        hi = k.astype(BF16)
        lo = (k - hi.astype(F32)).astype(BF16)
        return (jnp.dot(f, hi, preferred_element_type=F32) + jnp.dot(f, lo, preferred_element_type=F32))
    o_ref[0, 0] = (two_term(fa_ref[...], ka_ref[0]) + two_term(fb_ref[...], kb_ref[0])).astype(o_ref.dtype)


def filter_spectra(klag, fwd_g, c, *, tc=256):
    n_ord, two_l, d = klag.shape
    n_delta = two_l // c - 1
    fa, fb = fwd_g[:, :c], fwd_g[:, c:]
    return pl.pallas_call(
        _spectra_kernel,
        out_shape=jax.ShapeDtypeStruct((n_ord, n_delta, 2 * c, d), BF16),
        grid=(n_ord, n_delta, d // tc),
        in_specs=[pl.BlockSpec((2 * c, c), lambda o, dl, j: (0, 0)),
                  pl.BlockSpec((2 * c, c), lambda o, dl, j: (0, 0)),
                  pl.BlockSpec((1, c, tc), lambda o, dl, j: (o, dl, j)),
                  pl.BlockSpec((1, c, tc), lambda o, dl, j: (o, dl + 1, j))],
        out_specs=pl.BlockSpec((1, 1, 2 * c, tc), lambda o, dl, j: (o, dl, 0, j)),
        compiler_params=_cparams(("parallel", "parallel", "parallel")),
        name="filter_spectra",
    )(fa, fb, klag, klag)


def _longconv_kernel(u_ref, gate_ref, sk_ref, skip_ref, fu_ref, inv_ref, o_ref, spec_ref, y_ref, *, c, n_ch, order):
    for j in range(n_ch):
        spec_ref[j] = jnp.dot(fu_ref[...], u_ref[0, j * c:(j + 1) * c, :],
                              preferred_element_type=F32).astype(BF16)
    skip = skip_ref[order:order + 1, :]
    for i in range(n_ch):
        top = None
        bot = None
        for j in range(n_ch):
            dl = i - j + n_ch - 1
            xr = spec_ref[j, :c, :]
            xi = spec_ref[j, c:, :]
            gr = sk_ref[0, dl, :c, :]
            gi = sk_ref[0, dl, c:, :]
            t = xr * gr - xi * gi
            b = xr * gi + xi * gr
            top = t if top is None else top + t
            bot = b if bot is None else bot + b
        y_ref[i, :c, :] = top
        y_ref[i, c:, :] = bot
        y = jnp.dot(inv_ref[...], y_ref[i], preferred_element_type=F32)
        rows = slice(i * c, (i + 1) * c)
        u = u_ref[0, rows, :].astype(F32)
        o_ref[0, rows, :] = (gate_ref[0, rows, :].astype(F32) * (y + u * skip)).astype(o_ref.dtype)


def long_conv_gate(u_src, u_col0, gate_src, gate_col0, spectra, order, skip, fwd_u, inv, c, *, tc=512):
    b, n, _ = u_src.shape
    d = spectra.shape[3]
    n_ch = n // c
    n_delta = spectra.shape[1]
    nj = d // tc
    return pl.pallas_call(
        functools.partial(_longconv_kernel, c=c, n_ch=n_ch, order=order),
        out_shape=jax.ShapeDtypeStruct((b, n, d), BF16),
        grid=(nj, b),
        in_specs=[pl.BlockSpec((1, n, tc), lambda j, bi: (bi, 0, u_col0 * nj + j)),
                  pl.BlockSpec((1, n, tc), lambda j, bi: (bi, 0, gate_col0 * nj + j)),
                  pl.BlockSpec((1, n_delta, 2 * c, tc), lambda j, bi: (order, 0, 0, j)),
                  pl.BlockSpec((HY_ORDER, tc), lambda j, bi: (0, j)),
                  pl.BlockSpec((2 * c, c), lambda j, bi: (0, 0)),
                  pl.BlockSpec((c, 2 * c), lambda j, bi: (0, 0))],
        out_specs=pl.BlockSpec((1, n, tc), lambda j, bi: (bi, 0, j)),
        scratch_shapes=[pltpu.VMEM((n_ch, 2 * c, tc), BF16), pltpu.VMEM((n_ch, 2 * c, tc), BF16)],
        compiler_params=_cparams(("parallel", "parallel")),
        name="long_conv_gate",
    )(u_src, gate_src, spectra, skip.astype(F32), fwd_u, inv)


def _router_kernel(h_ref, w_ref, o_ref):
    logits = _dot3(h_ref[...], w_ref[...])
    lane = lax.broadcasted_iota(jnp.int32, logits.shape, 1).astype(F32)
    lg = jnp.where(lane < N_EXPERTS, logits, -jnp.inf)
    m1 = lg.max(axis=-1, keepdims=True)
    e1 = jnp.where(lg == m1, lane, float(LANES)).min(axis=-1, keepdims=True)
    lg2 = jnp.where(lane == e1, -jnp.inf, lg)
    m2 = lg2.max(axis=-1, keepdims=True)
    e2 = jnp.where(lg2 == m2, lane, float(LANES)).min(axis=-1, keepdims=True)
    t = jnp.exp(m2 - m1)
    p1 = 1.0 / (1.0 + t)
    p2 = t / (1.0 + t)
    o_ref[...] = jnp.where(lane == 0, e1, jnp.where(lane == 1, e2, jnp.where(lane == 2, p1,
                           jnp.where(lane == 3, p2, 0.0))))


def moe_route(h, router_w, *, tm=512):
    t, d = h.shape
    w = jnp.pad(router_w.astype(F32), ((0, 0), (0, LANES - N_EXPERTS)))
    return pl.pallas_call(
        _router_kernel,
        out_shape=jax.ShapeDtypeStruct((t, LANES), F32),
        grid=(t // tm,),
        in_specs=[pl.BlockSpec((tm, d), lambda i: (i, 0)), pl.BlockSpec((d, LANES), lambda i: (0, 0))],
        out_specs=pl.BlockSpec((tm, LANES), lambda i: (i, 0)),
        compiler_params=_cparams(("parallel",)),
        name="moe_router",
    )(h, w)


def _row_copy(src_ref, dst_ref, sem, src_row, dst_row):
    return pltpu.make_async_copy(src_ref.at[pl.ds(src_row, 1)], dst_ref.at[pl.ds(dst_row, 1)], sem)


def _load_indices(idx_vmem_ref, idx_smem, isem):
    cp = pltpu.make_async_copy(idx_vmem_ref, idx_smem, isem)
    cp.start()
    cp.wait()


def _issue_rows(idx_smem, src_ref, buf_ref, sem, n):
    def issue(blk, carry):
        for u in range(GATHER_UNROLL):
            r = blk * GATHER_UNROLL + u
            _row_copy(src_ref, buf_ref, sem, idx_smem[0, r], r).start()
        return carry

    lax.fori_loop(0, n // GATHER_UNROLL, issue, 0)


def _wait_rows(src_ref, buf_ref, sem, n):
    pltpu.make_async_copy(src_ref.at[pl.ds(0, n)], buf_ref, sem).wait()


def _dispatch_kernel(nu_ref, idx_ref, src_ref, o_ref, idx_smem, buf_ref, sem, isem, *, tm):
    used = pl.program_id(0) < nu_ref[0]

    @pl.when(used)
    def _():
        _load_indices(idx_ref.at[0], idx_smem, isem)
        _issue_rows(idx_smem, src_ref, buf_ref, sem, tm)
        _wait_rows(src_ref, buf_ref, sem, tm)
        o_ref[...] = buf_ref[...].astype(o_ref.dtype)

    @pl.when(jnp.logical_not(used))
    def _():
        o_ref[...] = jnp.zeros_like(o_ref)


def moe_dispatch(h, buf_tok, n_used, *, tm=MOE_TILE):
    _, d = h.shape
    n_rows = buf_tok.shape[0]
    idx = buf_tok.reshape(n_rows // tm, 1, tm)
    return pl.pallas_call(
        functools.partial(_dispatch_kernel, tm=tm),
        out_shape=jax.ShapeDtypeStruct((n_rows, d), BF16),
        grid_spec=pltpu.PrefetchScalarGridSpec(
            num_scalar_prefetch=1,
            grid=(n_rows // tm,),
            in_specs=[pl.BlockSpec((1, 1, tm), lambda i, nu: (i, 0, 0)), pl.BlockSpec(memory_space=pl.ANY)],
            out_specs=pl.BlockSpec((tm, d), lambda i, nu: (i, 0)),
            scratch_shapes=[pltpu.SMEM((1, tm), jnp.int32), pltpu.VMEM((tm, d), F32),
                            pltpu.SemaphoreType.DMA, pltpu.SemaphoreType.DMA],
        ),
        compiler_params=_cparams(("arbitrary",)),
        name="moe_dispatch",
    )(n_used, idx, h)


def _combine_kernel(i0_ref, i1_ref, yb_ref, slab_ref, o_ref, idx0_smem, idx1_smem, buf0_ref, buf1_ref,
                    sem0, sem1, isem, *, tm):
    _load_indices(i0_ref.at[0], idx0_smem, isem)
    _issue_rows(idx0_smem, yb_ref, buf0_ref, sem0, tm)
    _load_indices(i1_ref.at[0], idx1_smem, isem)
    _issue_rows(idx1_smem, yb_ref, buf1_ref, sem1, tm)
    _wait_rows(yb_ref, buf0_ref, sem0, tm)
    _wait_rows(yb_ref, buf1_ref, sem1, tm)
    o_ref[...] = buf0_ref[...] * slab_ref[:, 2:3] + buf1_ref[...] * slab_ref[:, 3:4]


def moe_combine(yb, pos0, pos1, slab, *, tm=256):
    t = pos0.shape[0]
    d = yb.shape[1]
    idx_spec = pl.BlockSpec((1, 1, tm), lambda i: (i, 0, 0))
    return pl.pallas_call(
        functools.partial(_combine_kernel, tm=tm),
        out_shape=jax.ShapeDtypeStruct((t, d), F32),
        grid=(t // tm,),
        in_specs=[idx_spec, idx_spec, pl.BlockSpec(memory_space=pl.ANY),
                  pl.BlockSpec((tm, LANES), lambda i: (i, 0))],
        out_specs=pl.BlockSpec((tm, d), lambda i: (i, 0)),
        scratch_shapes=[pltpu.SMEM((1, tm), jnp.int32), pltpu.SMEM((1, tm), jnp.int32),
                        pltpu.VMEM((tm, d), F32), pltpu.VMEM((tm, d), F32),
                        pltpu.SemaphoreType.DMA, pltpu.SemaphoreType.DMA, pltpu.SemaphoreType.DMA],
        compiler_params=_cparams(("arbitrary",)),
        name="moe_combine",
    )(pos0.reshape(t // tm, 1, tm), pos1.reshape(t // tm, 1, tm), yb, slab)


def moe_routing_tables(slab, tile):
    t = slab.shape[0]
    flat_e = slab[:, :2].astype(jnp.int32).reshape(-1)
    n_assign = flat_e.shape[0]
    onehot = (flat_e[:, None] == jnp.arange(N_EXPERTS)[None, :]).astype(jnp.int32)
    csum = jnp.cumsum(onehot, axis=0)
    counts = csum[-1]
    rank = jnp.take_along_axis(csum, flat_e[:, None], axis=1)[:, 0] - 1
    padded = (counts + tile - 1) // tile * tile
    pad_end = jnp.cumsum(padded)
    pad_start = pad_end - padded
    dest = (pad_start[flat_e] + rank).astype(jnp.int32)
    n_blocks = (n_assign + N_EXPERTS * (tile - 1) + tile - 1) // tile
    buf_tok = jnp.zeros((n_blocks * tile,), jnp.int32).at[dest].set(
        jnp.arange(n_assign, dtype=jnp.int32) // 2, unique_indices=True)
    block_e = jnp.minimum(jnp.searchsorted(pad_end, jnp.arange(n_blocks) * tile, side='right'),
                          N_EXPERTS - 1).astype(jnp.int32)
    dest = dest.reshape(t, 2)
    n_used = (pad_end[-1:] // tile).astype(jnp.int32)
    return buf_tok, block_e, n_used, dest[:, 0], dest[:, 1]


def moe_swiglu(h, router_w, w1, w3, w2, layer):
    slab = moe_route(h, router_w)
    buf_tok, block_e, n_used, pos0, pos1 = moe_routing_tables(slab, MOE_TILE)
    block_e = block_e + layer * N_EXPERTS
    xb = moe_dispatch(h, buf_tok, n_used)
    mid = swiglu_up(xb, w1, w3, block_e, tm=MOE_TILE, tn=512, n_used=n_used)
    yb = gmm(mid, w2, block_e, tm=MOE_TILE, tn=512, out_dtype=F32, n_used=n_used)
    return moe_combine(yb, pos0, pos1, slab)


def _row_tile(k):
    return 1024 if k <= 2048 else 512


def _dense_ids(n_rows, tm, idx):
    return jnp.full((n_rows // tm,), idx, jnp.int32)


def _col_tile(k, tm, n_out, first_col, out_bytes):
    budget = VMEM_LIMIT - 12 * 1024 * 1024
    for tn in (1024, 768, 512, 256):
        if n_out % tn or first_col % tn:
            continue
        working = 2 * k * tn * 4 + k * tn * 2 + 2 * tm * k * 2 + 2 * tm * tn * out_bytes + tm * tn * 4
        if working <= budget:
            return tn
    raise ValueError("no column tile fits")


def _dense(a3, w, idx, *, bias=None, out_dtype=BF16, first_col=0, n_out=None):
    b, s, k = a3.shape
    m = b * s
    tm = min(_row_tile(k), m)
    n_out = w.shape[2] if n_out is None else n_out
    tn = _col_tile(k, tm, n_out, first_col, jnp.dtype(out_dtype).itemsize)
    out = gmm(a3.reshape(m, k), w, _dense_ids(m, tm, idx), bias, tm=tm, tn=tn, out_dtype=out_dtype,
              col_off=first_col // tn, n_out=n_out)
    return out.reshape(b, s, -1)


def _dense_swiglu(h3, w1, w3, w2, idx):
    b, s, k = h3.shape
    m = b * s
    tm = min(1024, m)
    mid = swiglu_up(h3.reshape(m, k), w1, w3, _dense_ids(m, tm, idx), tm=tm, tn=512)
    tm2 = min(_row_tile(mid.shape[1]), m)
    return gmm(mid, w2, _dense_ids(m, tm2, idx), tm=tm2, tn=512, out_dtype=BF16).reshape(b, s, -1)


def _attention_layer(h, hc, i, w_in, w_out, q_g, k_g, bias, cos2, sin2, ctx_out):
    d = h.shape[-1]
    n_na = d // (2 * HEAD_DIM)
    n_q = n_na
    n_kv = max(1, n_q // 4)
    group = n_q // n_kv
    c_bq, c_ak, c_av = n_na, n_na + n_q, 2 * n_na + n_q
    c_bk, c_bv = 3 * n_na + n_q, 3 * n_na + n_q + n_kv
    q_cols = (n_na + n_q) * HEAD_DIM
    proj = _dense(h, w_in, i)
    if ctx_out:
        projc = _dense(hc, w_in, i)
        shift = 0
    else:
        projc = _dense(hc, w_in, i, first_col=q_cols, n_out=w_in.shape[2] - q_cols)
        shift = n_na + n_q
    qb = head_prep(proj, c_bq, n_q, q_g, cos2, sin2, rope=True)
    kb = head_prep(proj, c_bk, n_kv, k_g, cos2, sin2, rope=True)
    kbc = head_prep(projc, c_bk - shift, n_kv, k_g, cos2, sin2, rope=False)
    o_a = na_attention(proj, projc, n_na, 0, c_ak, c_av, c_ak - shift, c_av - shift, bias)
    o_b = full_attention(qb, 0, n_kv, group, [(kbc, 0), (kb, 0)], [(projc, c_bv - shift), (proj, c_bv)])
    y = _dense(jnp.concatenate([o_a, o_b], axis=-1), w_out, i)
    if not ctx_out:
        return y, None
    qbc = head_prep(projc, c_bq, n_q, q_g, cos2, sin2, rope=False)
    oc_a = full_attention(projc, 0, n_na, 1, [(projc, c_ak)], [(projc, c_av)])
    oc_b = full_attention(qbc, 0, n_kv, group, [(kbc, 0)], [(projc, c_bv)])
    yc = _dense(jnp.concatenate([oc_a, oc_b], axis=-1), w_out, i)
    return y, yc


def _hyena_layer(h, i, p):
    (w_in, b_in, conv_w, conv_b, f_w1, f_b1, f_w2, f_b2, f_w3, freq, skip, w_out, b_out) = p
    n_tok = h.shape[1]
    c = min(HY_CHUNK, n_tok)
    u0 = _dense(h, w_in, i, bias=b_in[i].reshape(1, -1))
    u = short_conv(u0, conv_w[i], conv_b[i])
    fwd_u, fwd_g, inv = dft_matrices(c)
    klag = hyena_lag_filters(n_tok, f_w1[i], f_b1[i], f_w2[i], f_b2[i], f_w3[i], freq[i])
    spectra = filter_spectra(klag, fwd_g, c)
    z = long_conv_gate(u, 0, u, 1, spectra, 0, skip[i], fwd_u, inv, c)
    z = long_conv_gate(z, 0, u, 2, spectra, 1, skip[i], fwd_u, inv, c)
    return _dense(z, w_out, i, bias=b_out[i].reshape(1, -1))


def kernel(x, c, ctx, c_ctx, mod_w, mod_b, norm_g, att_w_in, att_w_out, att_q_norm_g, att_k_norm_g, na_rpb,
           ffn_w1, ffn_w3, ffn_w2, hy_w_in, hy_b_in, hy_conv_w, hy_conv_b, hy_f_w1, hy_f_b1, hy_f_w2, hy_f_b2,
           hy_f_w3, hy_freq, hy_skip, hy_w_out, hy_b_out, moe_router, moe_w1, moe_w3, moe_w2):
    depth = mod_w.shape[0]
    b, s, d = x.shape
    n_ctx = ctx.shape[1]
    d_ff = moe_w1.shape[-1]
    cos2, sin2 = rope_tables(s)
    moe_w1f = moe_w1.reshape(-1, d, d_ff)
    moe_w3f = moe_w3.reshape(-1, d, d_ff)
    moe_w2f = moe_w2.reshape(-1, d_ff, d)
    hy = (hy_w_in, hy_b_in, hy_conv_w, hy_conv_b, hy_f_w1, hy_f_b1, hy_f_w2, hy_f_b2, hy_f_w3, hy_freq,
          hy_skip, hy_w_out, hy_b_out)

    n_mod_rows = 16
    cond = jnp.concatenate([c, c_ctx[None, :], jnp.zeros((n_mod_rows - b - 1, d), F32)], axis=0)

    lat_mods, ctx_mods = [], []
    for l in range(depth):
        mod = gmm(cond, mod_w, _dense_ids(n_mod_rows, n_mod_rows, l), mod_b[l].reshape(1, -1),
                  tm=n_mod_rows, tn=1024, out_dtype=F32, silu_in=True)
        m_lat = mod[:b].reshape(b, 6, 1, d)
        lat_mods.append([m_lat[:, k] for k in range(6)])
        m_ctx = jnp.broadcast_to(mod[b].reshape(1, 6, 1, d), (b, 6, 1, d))
        ctx_mods.append([m_ctx[:, k] for k in range(6)])

    xc = ctx
    h = modulate(x, norm_g[0, 0], lat_mods[0][0], lat_mods[0][1], out_dtype=BF16)
    for l in range(depth):
        even = l % 2 == 0
        i = l // 2
        ctx_out = any(m % 2 == 0 for m in range(l + 1, depth))
        ctx_in = even or ctx_out
        sh1, sc1, g1, sh2, sc2, g2 = lat_mods[l]
        csh1, csc1, cg1, csh2, csc2, cg2 = ctx_mods[l]

        hc = modulate(xc, norm_g[l, 0], csh1, csc1, out_dtype=BF16) if ctx_in else None
        if even:
            bias = na_bias_table(na_rpb[i], s // GRID_W)
            y, yc = _attention_layer(h, hc, i, att_w_in, att_w_out, att_q_norm_g[i], att_k_norm_g[i], bias,
                                     cos2, sin2, ctx_out)
        else:
            y = _hyena_layer(h, i, hy)
            yc = _hyena_layer(hc, i, hy) if ctx_out else None
        ffn_in = BF16 if even else F32
        x, h = residual_modulate(x, y, norm_g[l, 1], g1, norm_g[l, 2], sh2, sc2, out_dtype=ffn_in)
        if ctx_out:
            xc = residual(xc, yc, norm_g[l, 1], cg1)
            hc = modulate(xc, norm_g[l, 2], csh2, csc2, out_dtype=ffn_in)
        y_off = yc_off = 0
        if even:
            y = _dense_swiglu(h, ffn_w1, ffn_w3, ffn_w2, i)
            yc = _dense_swiglu(hc, ffn_w1, ffn_w3, ffn_w2, i) if ctx_out else None
        elif ctx_out and (b * n_ctx) % s == 0:
            tokens = jnp.concatenate([h.reshape(b * s, d), hc.reshape(b * n_ctx, d)], axis=0)
            y_all = moe_swiglu(tokens, moe_router[i], moe_w1f, moe_w3f, moe_w2f, i)
            y = y_all.reshape(-1, s, d)
            yc = y_all.reshape(-1, n_ctx, d)
            yc_off = b * s // n_ctx
        else:
            y = moe_swiglu(h.reshape(b * s, d), moe_router[i], moe_w1f, moe_w3f, moe_w2f, i).reshape(b, s, d)
            yc = None
            if ctx_out:
                yc = moe_swiglu(hc.reshape(b * n_ctx, d), moe_router[i], moe_w1f, moe_w3f, moe_w2f,
                                i).reshape(b, n_ctx, d)
        if l + 1 < depth:
            x, h = residual_modulate(x, y, norm_g[l, 3], g2, norm_g[l + 1, 0], lat_mods[l + 1][0],
                                     lat_mods[l + 1][1], out_dtype=BF16, y_off=y_off)
        else:
            x = residual(x, y, norm_g[l, 3], g2, y_off=y_off)
        if ctx_out:
            xc = residual(xc, yc, norm_g[l, 3], cg2, y_off=yc_off)
    return x
```

```python
import functools
import math

import jax
import jax.numpy as jnp
import numpy as np
from jax import lax
from jax.experimental import pallas as pl
from jax.experimental.pallas import tpu as pltpu

F32 = jnp.float32
BF16 = jnp.bfloat16
HIGHEST = lax.Precision.HIGHEST

GRID_W = 64
HEAD_DIM = 128
NA_WIN_ROWS = 8
NA_WIN_COLS = 16
ROPE_BASE = 10000.0
HY_ORDER = 2
HY_SHORT = 3
HY_EMB_DIM = 33
HY_TARGET = 1e-2
HY_FAST_DECAY = 0.3
HY_SLOW_DECAY = 1.5
HY_MOD_SHIFT = 0.05
N_EXPERTS = 8
NORM_EPS = 1e-6
NEG_INF = -1e30

LANES = 128
V7X_VMEM_BYTES = 64 * 1024 * 1024
VMEM_LIMIT = 56 * 1024 * 1024

ROW_TILE = 512
MOE_TILE = 512
GATHER_UNROLL = 8
HY_CHUNK = 512
NA_QROWS = 4
NA_KROWS = 12


def _cparams(sem):
    return pltpu.CompilerParams(dimension_semantics=sem, vmem_limit_bytes=VMEM_LIMIT)


def _weights_changed(be_ref, i):
    prev = be_ref[jnp.maximum(i - 1, 0)]
    return (i == 0) | (be_ref[i] != prev)


def _gmm_kernel(be_ref, nu_ref, *refs, n_a, silu_in):
    a_refs = refs[:n_a]
    w_ref, b_ref, o_ref, wbf_ref = refs[n_a:]
    i = pl.program_id(1)
    used = i < nu_ref[0]

    @pl.when(_weights_changed(be_ref, i))
    def _():
        wbf_ref[...] = w_ref[0].astype(BF16)

    @pl.when(used)
    def _():
        acc = b_ref[...]
        k0 = 0
        for a_ref in a_refs:
            a = a_ref[...]
            if silu_in:
                a = a.astype(F32)
                a = a * jax.nn.sigmoid(a)
            k1 = k0 + a.shape[1]
            acc = acc + jnp.dot(a.astype(BF16), wbf_ref[k0:k1, :], preferred_element_type=F32)
            k0 = k1
        o_ref[...] = acc.astype(o_ref.dtype)

    @pl.when(jnp.logical_not(used))
    def _():
        o_ref[...] = jnp.zeros_like(o_ref)


def _all_used(block_e):
    return jnp.full((1,), block_e.shape[0], jnp.int32)


def gmm(a, w, block_e, bias=None, *, tm, tn, out_dtype, col_off=0, n_out=None, silu_in=False, n_used=None):
    a_parts = tuple(a) if isinstance(a, (tuple, list)) else (a,)
    m = a_parts[0].shape[0]
    k = sum(p.shape[1] for p in a_parts)
    assert k == w.shape[1]
    n_out = w.shape[2] if n_out is None else n_out
    if bias is None:
        bias = jnp.zeros((1, w.shape[2]), F32)
    n_used = _all_used(block_e) if n_used is None else n_used
    grid = (n_out // tn, m // tm)
    a_specs = [pl.BlockSpec((tm, p.shape[1]), lambda j, i, be, nu: (jnp.minimum(i, nu[0] - 1), 0))
               for p in a_parts]
    return pl.pallas_call(
        functools.partial(_gmm_kernel, n_a=len(a_parts), silu_in=silu_in),
        out_shape=jax.ShapeDtypeStruct((m, n_out), out_dtype),
        grid_spec=pltpu.PrefetchScalarGridSpec(
            num_scalar_prefetch=2,
            grid=grid,
            in_specs=a_specs + [
                pl.BlockSpec((1, k, tn), lambda j, i, be, nu: (be[i], 0, j + col_off)),
                pl.BlockSpec((1, tn), lambda j, i, be, nu: (0, j + col_off)),
            ],
            out_specs=pl.BlockSpec((tm, tn), lambda j, i, be, nu: (i, j)),
            scratch_shapes=[pltpu.VMEM((k, tn), BF16)],
        ),
        compiler_params=_cparams(("arbitrary", "arbitrary")),
        name="gmm",
    )(block_e, n_used, *a_parts, w, bias)


def _up_kernel(be_ref, nu_ref, a_ref, w1_ref, w3_ref, o_ref, w1bf_ref, w3bf_ref):
    i = pl.program_id(1)
    used = i < nu_ref[0]

    @pl.when(_weights_changed(be_ref, i))
    def _():
        w1bf_ref[...] = w1_ref[0].astype(BF16)
        w3bf_ref[...] = w3_ref[0].astype(BF16)

    @pl.when(used)
    def _():
        a = a_ref[...]
        g = jnp.dot(a, w1bf_ref[...], preferred_element_type=F32)
        u = jnp.dot(a, w3bf_ref[...], preferred_element_type=F32)
        o_ref[...] = (g * jax.nn.sigmoid(g) * u).astype(o_ref.dtype)

    @pl.when(jnp.logical_not(used))
    def _():
        o_ref[...] = jnp.zeros_like(o_ref)


def swiglu_up(a, w1, w3, block_e, *, tm, tn, n_used=None):
    m, k = a.shape
    n = w1.shape[2]
    n_used = _all_used(block_e) if n_used is None else n_used
    grid = (n // tn, m // tm)
    return pl.pallas_call(
        _up_kernel,
        out_shape=jax.ShapeDtypeStruct((m, n), BF16),
        grid_spec=pltpu.PrefetchScalarGridSpec(
            num_scalar_prefetch=2,
            grid=grid,
            in_specs=[
                pl.BlockSpec((tm, k), lambda j, i, be, nu: (jnp.minimum(i, nu[0] - 1), 0)),
                pl.BlockSpec((1, k, tn), lambda j, i, be, nu: (be[i], 0, j)),
                pl.BlockSpec((1, k, tn), lambda j, i, be, nu: (be[i], 0, j)),
            ],
            out_specs=pl.BlockSpec((tm, tn), lambda j, i, be, nu: (i, j)),
            scratch_shapes=[pltpu.VMEM((k, tn), BF16), pltpu.VMEM((k, tn), BF16)],
        ),
        compiler_params=_cparams(("arbitrary", "arbitrary")),
        name="swiglu_up",
    )(block_e, n_used, a, w1, w3)


def _rms(x, g):
    ms = jnp.mean(x * x, axis=-1, keepdims=True)
    return x * lax.rsqrt(ms + NORM_EPS) * g


def _mod_kernel(x_ref, g_ref, sh_ref, sc_ref, o_ref):
    y = _rms(x_ref[0], g_ref[...])
    o_ref[0] = (y * (1.0 + sc_ref[0]) + sh_ref[0]).astype(o_ref.dtype)


def modulate(x, g, shift, scale, *, out_dtype, tr=ROW_TILE):
    b, s, d = x.shape
    tr = min(tr, s)
    row = pl.BlockSpec((1, tr, d), lambda bi, si: (bi, si, 0))
    vec = pl.BlockSpec((1, 1, d), lambda bi, si: (bi, 0, 0))
    return pl.pallas_call(
        _mod_kernel,
        out_shape=jax.ShapeDtypeStruct((b, s, d), out_dtype),
        grid=(b, s // tr),
        in_specs=[row, pl.BlockSpec((1, d), lambda bi, si: (0, 0)), vec, vec],
        out_specs=row,
        compiler_params=_cparams(("parallel", "parallel")),
        name="modulate",
    )(x, g.reshape(1, d), shift, scale)


def _resid_kernel(x_ref, y_ref, g_ref, gate_ref, o_ref):
    y = _rms(y_ref[0].astype(F32), g_ref[...])
    o_ref[0] = x_ref[0] + gate_ref[0] * y


def residual(x, y, g, gate, *, tr=ROW_TILE, y_off=0):
    b, s, d = x.shape
    tr = min(tr, s)
    row = pl.BlockSpec((1, tr, d), lambda bi, si: (bi, si, 0))
    y_row = pl.BlockSpec((1, tr, d), lambda bi, si: (bi + y_off, si, 0))
    vec = pl.BlockSpec((1, 1, d), lambda bi, si: (bi, 0, 0))
    return pl.pallas_call(
        _resid_kernel,
        out_shape=jax.ShapeDtypeStruct((b, s, d), F32),
        grid=(b, s // tr),
        in_specs=[row, y_row, pl.BlockSpec((1, d), lambda bi, si: (0, 0)), vec],
        out_specs=row,
        compiler_params=_cparams(("parallel", "parallel")),
        name="residual",
    )(x, y, g.reshape(1, d), gate)


def _resid_mod_kernel(x_ref, y_ref, gy_ref, gate_ref, gh_ref, sh_ref, sc_ref, xo_ref, ho_ref):
    x = x_ref[0] + gate_ref[0] * _rms(y_ref[0].astype(F32), gy_ref[...])
    xo_ref[0] = x
    ho_ref[0] = (_rms(x, gh_ref[...]) * (1.0 + sc_ref[0]) + sh_ref[0]).astype(ho_ref.dtype)


def residual_modulate(x, y, g_y, gate, g_h, shift, scale, *, out_dtype, tr=ROW_TILE, y_off=0):
    b, s, d = x.shape
    tr = min(tr, s)
    row = pl.BlockSpec((1, tr, d), lambda bi, si: (bi, si, 0))
    y_row = pl.BlockSpec((1, tr, d), lambda bi, si: (bi + y_off, si, 0))
    vec = pl.BlockSpec((1, 1, d), lambda bi, si: (bi, 0, 0))
    gain = pl.BlockSpec((1, d), lambda bi, si: (0, 0))
    return pl.pallas_call(
        _resid_mod_kernel,
        out_shape=(jax.ShapeDtypeStruct((b, s, d), F32), jax.ShapeDtypeStruct((b, s, d), out_dtype)),
        grid=(b, s // tr),
        in_specs=[row, y_row, gain, vec, gain, vec, vec],
        out_specs=(row, row),
        compiler_params=_cparams(("parallel", "parallel")),
        name="residual_modulate",
    )(x, y, g_y.reshape(1, d), gate, g_h.reshape(1, d), shift, scale)


def _prep_kernel(x_ref, g_ref, cos_ref, sin_ref, o_ref, *, rope, n_heads):
    for h in range(n_heads):
        cols = slice(h * HEAD_DIM, (h + 1) * HEAD_DIM)
        y = _rms(x_ref[0, :, cols].astype(F32), g_ref[...])
        if rope:
            rot = pltpu.roll(y, HEAD_DIM // 2, axis=1)
            y = y * cos_ref[...] + rot * sin_ref[...]
        o_ref[0, :, cols] = y.astype(o_ref.dtype)


def head_prep(src, col0, n_heads, g, cos2, sin2, *, rope):
    b, s, _ = src.shape
    ts = min(s, 512)
    width = n_heads * HEAD_DIM
    assert col0 % n_heads == 0
    return pl.pallas_call(
        functools.partial(_prep_kernel, rope=rope, n_heads=n_heads),
        out_shape=jax.ShapeDtypeStruct((b, s, width), BF16),
        grid=(b, s // ts),
        in_specs=[
            pl.BlockSpec((1, ts, width), lambda bi, si: (bi, si, col0 // n_heads)),
            pl.BlockSpec((1, HEAD_DIM), lambda bi, si: (0, 0)),
            pl.BlockSpec((ts, HEAD_DIM), lambda bi, si: (si, 0)),
            pl.BlockSpec((ts, HEAD_DIM), lambda bi, si: (si, 0)),
        ],
        out_specs=pl.BlockSpec((1, ts, width), lambda bi, si: (bi, si, 0)),
        compiler_params=_cparams(("parallel", "parallel")),
        name="head_prep",
    )(src, g.reshape(1, HEAD_DIM), cos2[:s], sin2[:s])


def _nt_dot(q, k):
    return lax.dot_general(q, k, (((1,), (1,)), ((), ())), preferred_element_type=F32)


def _with_ones(v):
    return jnp.concatenate([v, jnp.ones(v.shape, v.dtype)], axis=1)


def _softmax_pv(scores, values_ext, out_dtype):
    m = scores[0].max(axis=-1, keepdims=True)
    for s in scores[1:]:
        m = jnp.maximum(m, s.max(axis=-1, keepdims=True))
    ps = [jnp.exp((s - m).astype(BF16)) for s in scores]
    acc = jnp.dot(ps[0], values_ext[0], preferred_element_type=F32)
    for p, v in zip(ps[1:], values_ext[1:]):
        acc = acc + jnp.dot(p, v, preferred_element_type=F32)
    return (acc[:, :HEAD_DIM] / acc[:, HEAD_DIM:HEAD_DIM + 1]).astype(out_dtype)


def _scaled(q):
    return (q.astype(F32) * (HEAD_DIM ** -0.5)).astype(BF16)


def _attn_kernel(*refs, group, n_parts):
    q_ref = refs[0]
    k_refs = refs[1:1 + n_parts]
    v_refs = refs[1 + n_parts:1 + 2 * n_parts]
    o_ref = refs[1 + 2 * n_parts]
    ks = [r[0] for r in k_refs]
    vs = [_with_ones(r[0]) for r in v_refs]
    for g in range(group):
        cols = slice(g * HEAD_DIM, (g + 1) * HEAD_DIM)
        q = _scaled(q_ref[0, :, cols])
        scores = [_nt_dot(q, k) for k in ks]
        o_ref[0, :, cols] = _softmax_pv(scores, vs, o_ref.dtype)


def full_attention(q_src, q_col0, n_kv, group, k_parts, v_parts, *, tq=512):
    b, sq, _ = q_src.shape
    tq = min(tq, sq)
    gw = group * HEAD_DIM
    assert (q_col0 * HEAD_DIM) % gw == 0
    q_blk0 = q_col0 * HEAD_DIM // gw
    in_specs = [pl.BlockSpec((1, tq, gw), lambda bi, h, qi: (bi, qi, q_blk0 + h))]
    args = [q_src]
    for arr, c0 in list(k_parts) + list(v_parts):
        in_specs.append(pl.BlockSpec((1, arr.shape[1], HEAD_DIM),
                                     functools.partial(lambda bi, h, qi, c0: (bi, 0, c0 + h), c0=c0)))
        args.append(arr)
    return pl.pallas_call(
        functools.partial(_attn_kernel, group=group, n_parts=len(k_parts)),
        out_shape=jax.ShapeDtypeStruct((b, sq, n_kv * gw), BF16),
        grid=(b, n_kv, sq // tq),
        in_specs=in_specs,
        out_specs=pl.BlockSpec((1, tq, gw), lambda bi, h, qi: (bi, qi, h)),
        compiler_params=_cparams(("parallel", "parallel", "parallel")),
        name="full_attention",
    )(*args)


NA_MASKED = 2 * NA_WIN_ROWS - 1


def _na_bias(bias_ref, r0, start, rows):
    lane = lax.broadcasted_iota(jnp.int32, (GRID_W, 2 * GRID_W), 1)
    bands = []
    for i in range(NA_QROWS):
        r = r0 + i
        rs = min(max(r - NA_WIN_ROWS // 2, 0), rows - NA_WIN_ROWS)

        def block_id(j):
            kr = start + j
            return kr - r + NA_WIN_ROWS - 1 if rs <= kr < rs + NA_WIN_ROWS else NA_MASKED

        tiles = []
        for m in range(NA_KROWS // 2):
            d0, d1 = block_id(2 * m), block_id(2 * m + 1)
            tile = bias_ref[0, d0]
            if d1 != d0:
                tile = jnp.where(lane < GRID_W, tile, bias_ref[0, d1])
            tiles.append(tile)
        bands.append(jnp.concatenate(tiles, axis=1))
    return jnp.concatenate(bands, axis=0)


def _na_kernel(q_ref, k_ref, v_ref, kc_ref, vc_ref, bias_ref, o_ref, *, rows):
    n_steps = rows // NA_QROWS
    kc = kc_ref[0]
    vc = _with_ones(vc_ref[0])
    for step in range(n_steps):
        r0 = step * NA_QROWS
        start = min(max(r0 - NA_WIN_ROWS // 2, 0), rows - NA_KROWS)
        qs = slice(r0 * GRID_W, (r0 + NA_QROWS) * GRID_W)
        kslice = slice(start * GRID_W, (start + NA_KROWS) * GRID_W)
        q = _scaled(q_ref[0, qs, :])
        s_lat = _nt_dot(q, k_ref[0, kslice, :]) + _na_bias(bias_ref, r0, start, rows)
        s_ctx = _nt_dot(q, kc)
        o_ref[0, qs, :] = _softmax_pv([s_lat, s_ctx], [_with_ones(v_ref[0, kslice, :]), vc], o_ref.dtype)


def na_bias_blocks(rpb):
    n_dc = 2 * NA_WIN_COLS - 1
    col = np.arange(GRID_W)
    col_start = np.clip(col - NA_WIN_COLS // 2, 0, GRID_W - NA_WIN_COLS)
    col_ok = (col[None, :] >= col_start[:, None]) & (col[None, :] < col_start[:, None] + NA_WIN_COLS)
    dc = np.clip(col[None, :] - col[:, None] + NA_WIN_COLS - 1, 0, n_dc - 1)
    pick_dc = (dc[:, :, None] == np.arange(n_dc)).astype(np.float32)
    blocks = jnp.einsum('hde,qke->hdqk', rpb.astype(F32), pick_dc, precision=HIGHEST)
    blocks = jnp.where(col_ok[None, None], blocks, NEG_INF)
    masked = jnp.full((rpb.shape[0], 1, GRID_W, GRID_W), NEG_INF, F32)
    blocks = jnp.concatenate([blocks, masked], axis=1)
    return jnp.concatenate([blocks, blocks], axis=-1)


def na_attention(proj, projc, n_heads, q_col0, k_col0, v_col0, kc_col0, vc_col0, bias):
    b, s, _ = proj.shape
    n_ctx = projc.shape[1]
    rows = s // GRID_W
    assert rows % NA_QROWS == 0 and NA_KROWS % 2 == 0 and 2 * GRID_W == LANES
    lat = lambda c0: pl.BlockSpec((1, s, HEAD_DIM), lambda h, bi: (bi, 0, c0 + h))
    ctx = lambda c0: pl.BlockSpec((1, n_ctx, HEAD_DIM), lambda h, bi: (bi, 0, c0 + h))
    return pl.pallas_call(
        functools.partial(_na_kernel, rows=rows),
        out_shape=jax.ShapeDtypeStruct((b, s, n_heads * HEAD_DIM), BF16),
        grid=(n_heads, b),
        in_specs=[lat(q_col0), lat(k_col0), lat(v_col0), ctx(kc_col0), ctx(vc_col0),
                  pl.BlockSpec((1,) + bias.shape[1:], lambda h, bi: (h, 0, 0, 0))],
        out_specs=pl.BlockSpec((1, s, HEAD_DIM), lambda h, bi: (bi, 0, h)),
        compiler_params=_cparams(("parallel", "parallel")),
        name="na_attention",
    )(proj, proj, proj, projc, projc, bias)


def rope_tables(n_tok):
    t = jnp.arange(n_tok)
    row = (t // GRID_W).astype(F32)
    col = (t % GRID_W).astype(F32)
    n_freq = HEAD_DIM // 4
    inv_freq = ROPE_BASE ** (-jnp.arange(n_freq, dtype=F32) / n_freq)
    ang = jnp.concatenate([row[:, None] * inv_freq, col[:, None] * inv_freq], axis=-1)
    cos, sin = jnp.cos(ang), jnp.sin(ang)
    return jnp.concatenate([cos, cos], axis=-1), jnp.concatenate([-sin, sin], axis=-1)


def _shortconv_kernel(u_ref, w_ref, b_ref, o_ref):
    u = u_ref[0].astype(F32)
    n = u.shape[0]
    row = lax.broadcasted_iota(jnp.int32, u.shape, 0)
    prev = jnp.where(row == 0, 0.0, pltpu.roll(u, 1, axis=0))
    nxt = jnp.where(row == n - 1, 0.0, pltpu.roll(u, n - 1, axis=0))
    o = prev * w_ref[0:1, :] + u * w_ref[1:2, :] + nxt * w_ref[2:3, :] + b_ref[...]
    o_ref[0] = o.astype(o_ref.dtype)


def short_conv(u, w, bias, *, tc=256):
    b, n, c = u.shape
    return pl.pallas_call(
        _shortconv_kernel,
        out_shape=jax.ShapeDtypeStruct((b, n, c), BF16),
        grid=(b, c // tc),
        in_specs=[pl.BlockSpec((1, n, tc), lambda bi, j: (bi, 0, j)),
                  pl.BlockSpec((HY_SHORT, tc), lambda bi, j: (0, j)),
                  pl.BlockSpec((1, tc), lambda bi, j: (0, j))],
        out_specs=pl.BlockSpec((1, n, tc), lambda bi, j: (bi, 0, j)),
        compiler_params=_cparams(("parallel", "parallel")),
        name="short_conv",
    )(u, w, bias.reshape(1, c))


def _filter_mlp_kernel(z_ref, w1_ref, b1_ref, w2_ref, b2_ref, fr_ref, o_ref):
    fr = fr_ref[...]
    a = jnp.sin(fr * (jnp.dot(z_ref[...], w1_ref[...], precision=HIGHEST, preferred_element_type=F32) + b1_ref[...]))
    o_ref[...] = jnp.sin(fr * (jnp.dot(a, w2_ref[...], precision=HIGHEST, preferred_element_type=F32) + b2_ref[...]))


def _split_bf16(x):
    hi = x.astype(BF16)
    return hi, (x - hi.astype(F32)).astype(BF16)


def _dot3(a, w):
    a_hi, a_lo = _split_bf16(a)
    w_hi, w_lo = _split_bf16(w)
    return (jnp.dot(a_hi, w_hi, preferred_element_type=F32) + jnp.dot(a_lo, w_hi, preferred_element_type=F32)
            + jnp.dot(a_hi, w_lo, preferred_element_type=F32))


def _filter_kernel(a_ref, t_ref, w3f_ref, w3b_ref, dl_ref, o_ref, *, n_tok):
    a = a_ref[...]
    f_fwd = _dot3(a, w3f_ref[...])
    f_bwd = _dot3(a, w3b_ref[...])
    window = jnp.exp(-t_ref[...] * dl_ref[...]) + HY_MOD_SHIFT
    lag_row = lax.broadcasted_iota(jnp.int32, f_fwd.shape, 0)
    filt = jnp.where(lag_row >= n_tok, f_fwd, f_bwd) * window
    o_ref[0] = jnp.where(lag_row == 0, 0.0, filt)


def hyena_lag_filters(n_tok, f_w1, f_b1, f_w2, f_b2, f_w3, freq, *, tc=256):
    d = f_w3.shape[1] // (2 * HY_ORDER)
    hidden = f_w1.shape[1]
    bands = (HY_EMB_DIM - 1) // 2
    pos = jnp.abs(jnp.arange(2 * n_tok) - n_tok)
    pos = jnp.minimum(pos, n_tok - 1)
    t = jnp.linspace(0.0, 1.0, n_tok, dtype=F32)[pos][:, None]
    w = (2.0 * math.pi * jnp.arange(n_tok, dtype=F32) / n_tok)[pos][:, None]
    f = jnp.linspace(1e-4, bands - 1, bands, dtype=F32)[None, :]
    z = jnp.concatenate([t, jnp.cos(f * w), -jnp.sin(f * w)], axis=-1)
    z = jnp.pad(z, ((0, 0), (0, LANES - HY_EMB_DIM)))
    w1 = jnp.pad(f_w1.astype(F32), ((0, LANES - HY_EMB_DIM), (0, 0)))
    max_decay = math.log(HY_TARGET) / HY_FAST_DECAY
    min_decay = math.log(HY_TARGET) / HY_SLOW_DECAY
    deltas = jnp.abs(jnp.linspace(min_decay, max_decay, d, dtype=F32)).reshape(1, d)
    nj = d // tc
    n_lag = 2 * n_tok
    one = lambda shape: pl.BlockSpec(shape, lambda i: (0,) * len(shape))
    feats = pl.pallas_call(
        _filter_mlp_kernel,
        out_shape=jax.ShapeDtypeStruct((n_lag, hidden), F32),
        grid=(1,),
        in_specs=[one((n_lag, LANES)), one((LANES, hidden)), one((1, hidden)), one((hidden, hidden)),
                  one((1, hidden)), one((1, hidden))],
        out_specs=one((n_lag, hidden)),
        compiler_params=_cparams(("arbitrary",)),
        name="hyena_filter_mlp",
    )(z, w1, f_b1.reshape(1, hidden).astype(F32), f_w2.astype(F32), f_b2.reshape(1, hidden).astype(F32),
      freq.reshape(1, hidden).astype(F32))
    full = lambda shape: pl.BlockSpec(shape, lambda o, j: (0,) * len(shape))
    return pl.pallas_call(
        functools.partial(_filter_kernel, n_tok=n_tok),
        out_shape=jax.ShapeDtypeStruct((HY_ORDER, n_lag, d), F32),
        grid=(HY_ORDER, nj),
        in_specs=[full((n_lag, hidden)), full((n_lag, 1)),
                  pl.BlockSpec((hidden, tc), lambda o, j: (0, (2 * o) * nj + j)),
                  pl.BlockSpec((hidden, tc), lambda o, j: (0, (2 * o + 1) * nj + j)),
                  pl.BlockSpec((1, tc), lambda o, j: (0, j))],
        out_specs=pl.BlockSpec((1, n_lag, tc), lambda o, j: (o, 0, j)),
        compiler_params=_cparams(("parallel", "parallel")),
        name="hyena_filters",
    )(feats, t, f_w3.astype(F32), f_w3.astype(F32), deltas)


def _dft_angles(f, n, c):
    k = jnp.mod((2 * f + 1) * n, 4 * c).astype(F32)
    return k * (2.0 * math.pi / (4 * c))


def dft_matrices(c):
    f = jnp.arange(c)[:, None]
    ang_u = _dft_angles(f, jnp.arange(c)[None, :], c)
    fwd_u = jnp.concatenate([jnp.cos(ang_u), -jnp.sin(ang_u)], axis=0)
    ang_g = _dft_angles(f, jnp.arange(-c, c)[None, :], c)
    fwd_g = jnp.concatenate([jnp.cos(ang_g), -jnp.sin(ang_g)], axis=0)
    inv = jnp.concatenate([jnp.cos(ang_u).T, -jnp.sin(ang_u).T], axis=1) / c
    return fwd_u.astype(BF16), fwd_g.astype(BF16), inv.astype(BF16)


def _spectra_kernel(fa_ref, fb_ref, ka_ref, kb_ref, o_ref):
    acc = jnp.dot(fa_ref[...], ka_ref[0].astype(BF16), preferred_element_type=F32)
    acc = acc + jnp.dot(fb_ref[...], kb_ref[0].astype(BF16), preferred_element_type=F32)
    o_ref[0, 0] = acc.astype(o_ref.dtype)


def filter_spectra(klag, fwd_g, c, *, tc=1024):
    n_ord, two_l, d = klag.shape
    n_delta = two_l // c - 1
    fa, fb = fwd_g[:, :c], fwd_g[:, c:]
    return pl.pallas_call(
        _spectra_kernel,
        out_shape=jax.ShapeDtypeStruct((n_ord, n_delta, 2 * c, d), BF16),
        grid=(n_ord, n_delta, d // tc),
        in_specs=[pl.BlockSpec((2 * c, c), lambda o, dl, j: (0, 0)),
                  pl.BlockSpec((2 * c, c), lambda o, dl, j: (0, 0)),
                  pl.BlockSpec((1, c, tc), lambda o, dl, j: (o, dl, j)),
                  pl.BlockSpec((1, c, tc), lambda o, dl, j: (o, dl + 1, j))],
        out_specs=pl.BlockSpec((1, 1, 2 * c, tc), lambda o, dl, j: (o, dl, 0, j)),
        compiler_params=_cparams(("parallel", "parallel", "parallel")),
        name="filter_spectra",
    )(fa, fb, klag, klag)


def _longconv_kernel(u_ref, gate_ref, sk_ref, skip_ref, fu_ref, inv_ref, o_ref, spec_ref, y_ref, *, c, n_ch, order):
    for j in range(n_ch):
        spec_ref[j] = jnp.dot(fu_ref[...], u_ref[0, j * c:(j + 1) * c, :],
                              preferred_element_type=F32).astype(BF16)
    skip = skip_ref[order:order + 1, :]
    for i in range(n_ch):
        top = None
        bot = None
        for j in range(n_ch):
            dl = i - j + n_ch - 1
            xr = spec_ref[j, :c, :]
            xi = spec_ref[j, c:, :]
            gr = sk_ref[0, dl, :c, :]
            gi = sk_ref[0, dl, c:, :]
            t = xr * gr - xi * gi
            b = xr * gi + xi * gr
            top = t if top is None else top + t
            bot = b if bot is None else bot + b
        y_ref[i, :c, :] = top
        y_ref[i, c:, :] = bot
        y = jnp.dot(inv_ref[...], y_ref[i], preferred_element_type=F32)
        rows = slice(i * c, (i + 1) * c)
        u = u_ref[0, rows, :].astype(F32)
        o_ref[0, rows, :] = (gate_ref[0, rows, :].astype(F32) * (y + u * skip)).astype(o_ref.dtype)


def long_conv_gate(u_src, u_col0, gate_src, gate_col0, spectra, order, skip, fwd_u, inv, c, *, tc=512):
    b, n, _ = u_src.shape
    d = spectra.shape[3]
    n_ch = n // c
    n_delta = spectra.shape[1]
    nj = d // tc
    return pl.pallas_call(
        functools.partial(_longconv_kernel, c=c, n_ch=n_ch, order=order),
        out_shape=jax.ShapeDtypeStruct((b, n, d), BF16),
        grid=(nj, b),
        in_specs=[pl.BlockSpec((1, n, tc), lambda j, bi: (bi, 0, u_col0 * nj + j)),
                  pl.BlockSpec((1, n, tc), lambda j, bi: (bi, 0, gate_col0 * nj + j)),
                  pl.BlockSpec((1, n_delta, 2 * c, tc), lambda j, bi: (order, 0, 0, j)),
                  pl.BlockSpec((HY_ORDER, tc), lambda j, bi: (0, j)),
                  pl.BlockSpec((2 * c, c), lambda j, bi: (0, 0)),
                  pl.BlockSpec((c, 2 * c), lambda j, bi: (0, 0))],
        out_specs=pl.BlockSpec((1, n, tc), lambda j, bi: (bi, 0, j)),
        scratch_shapes=[pltpu.VMEM((n_ch, 2 * c, tc), BF16), pltpu.VMEM((n_ch, 2 * c, tc), BF16)],
        compiler_params=_cparams(("parallel", "parallel")),
        name="long_conv_gate",
    )(u_src, gate_src, spectra, skip.astype(F32), fwd_u, inv)


def _router_kernel(h_ref, w_ref, o_ref):
    logits = _dot3(h_ref[...], w_ref[...])
    lane = lax.broadcasted_iota(jnp.int32, logits.shape, 1).astype(F32)
    lg = jnp.where(lane < N_EXPERTS, logits, -jnp.inf)
    m1 = lg.max(axis=-1, keepdims=True)
    e1 = jnp.where(lg == m1, lane, float(LANES)).min(axis=-1, keepdims=True)
    lg2 = jnp.where(lane == e1, -jnp.inf, lg)
    m2 = lg2.max(axis=-1, keepdims=True)
    e2 = jnp.where(lg2 == m2, lane, float(LANES)).min(axis=-1, keepdims=True)
    t = jnp.exp(m2 - m1)
    p1 = 1.0 / (1.0 + t)
    p2 = t / (1.0 + t)
    o_ref[...] = jnp.where(lane == 0, e1, jnp.where(lane == 1, e2, jnp.where(lane == 2, p1,
                           jnp.where(lane == 3, p2, 0.0))))


def moe_route(h, router_w, *, tm=512):
    t, d = h.shape
    w = jnp.pad(router_w.astype(F32), ((0, 0), (0, LANES - N_EXPERTS)))
    return pl.pallas_call(
        _router_kernel,
        out_shape=jax.ShapeDtypeStruct((t, LANES), F32),
        grid=(t // tm,),
        in_specs=[pl.BlockSpec((tm, d), lambda i: (i, 0)), pl.BlockSpec((d, LANES), lambda i: (0, 0))],
        out_specs=pl.BlockSpec((tm, LANES), lambda i: (i, 0)),
        compiler_params=_cparams(("parallel",)),
        name="moe_router",
    )(h, w)


def _row_copy(src_ref, dst_ref, sem, src_row, dst_row):
    return pltpu.make_async_copy(src_ref.at[pl.ds(src_row, 1)], dst_ref.at[pl.ds(dst_row, 1)], sem)


def _load_indices(idx_vmem_ref, idx_smem, isem):
    cp = pltpu.make_async_copy(idx_vmem_ref, idx_smem, isem)
    cp.start()
    cp.wait()


def _issue_rows(idx_smem, src_ref, buf_ref, sem, n):
    def issue(blk, carry):
        for u in range(GATHER_UNROLL):
            r = blk * GATHER_UNROLL + u
            _row_copy(src_ref, buf_ref, sem, idx_smem[0, r], r).start()
        return carry

    lax.fori_loop(0, n // GATHER_UNROLL, issue, 0)


def _wait_rows(src_ref, buf_ref, sem, n):
    pltpu.make_async_copy(src_ref.at[pl.ds(0, n)], buf_ref, sem).wait()


def _dispatch_kernel(nu_ref, idx_ref, src_ref, o_ref, idx_smem, buf_ref, sem, isem, *, tm):
    used = pl.program_id(0) < nu_ref[0]

    @pl.when(used)
    def _():
        _load_indices(idx_ref.at[0], idx_smem, isem)
        _issue_rows(idx_smem, src_ref, buf_ref, sem, tm)
        _wait_rows(src_ref, buf_ref, sem, tm)
        o_ref[...] = buf_ref[...].astype(o_ref.dtype)

    @pl.when(jnp.logical_not(used))
    def _():
        o_ref[...] = jnp.zeros_like(o_ref)


def moe_dispatch(h, buf_tok, n_used, *, tm=MOE_TILE):
    _, d = h.shape
    n_rows = buf_tok.shape[0]
    idx = buf_tok.reshape(n_rows // tm, 1, tm)
    return pl.pallas_call(
        functools.partial(_dispatch_kernel, tm=tm),
        out_shape=jax.ShapeDtypeStruct((n_rows, d), BF16),
        grid_spec=pltpu.PrefetchScalarGridSpec(
            num_scalar_prefetch=1,
            grid=(n_rows // tm,),
            in_specs=[pl.BlockSpec((1, 1, tm), lambda i, nu: (i, 0, 0)), pl.BlockSpec(memory_space=pl.ANY)],
            out_specs=pl.BlockSpec((tm, d), lambda i, nu: (i, 0)),
            scratch_shapes=[pltpu.SMEM((1, tm), jnp.int32), pltpu.VMEM((tm, d), F32),
                            pltpu.SemaphoreType.DMA, pltpu.SemaphoreType.DMA],
        ),
        compiler_params=_cparams(("arbitrary",)),
        name="moe_dispatch",
    )(n_used, idx, h)


def _combine_kernel(i0_ref, i1_ref, yb_ref, slab_ref, o_ref, idx0_smem, idx1_smem, buf0_ref, buf1_ref,
                    sem0, sem1, isem, *, tm):
    _load_indices(i0_ref.at[0], idx0_smem, isem)
    _issue_rows(idx0_smem, yb_ref, buf0_ref, sem0, tm)
    _load_indices(i1_ref.at[0], idx1_smem, isem)
    _issue_rows(idx1_smem, yb_ref, buf1_ref, sem1, tm)
    _wait_rows(yb_ref, buf0_ref, sem0, tm)
    _wait_rows(yb_ref, buf1_ref, sem1, tm)
    o_ref[...] = buf0_ref[...] * slab_ref[:, 2:3] + buf1_ref[...] * slab_ref[:, 3:4]


def moe_combine(yb, pos0, pos1, slab, *, tm=512):
    t = pos0.shape[0]
    d = yb.shape[1]
    idx_spec = pl.BlockSpec((1, 1, tm), lambda i: (i, 0, 0))
    return pl.pallas_call(
        functools.partial(_combine_kernel, tm=tm),
        out_shape=jax.ShapeDtypeStruct((t, d), F32),
        grid=(t // tm,),
        in_specs=[idx_spec, idx_spec, pl.BlockSpec(memory_space=pl.ANY),
                  pl.BlockSpec((tm, LANES), lambda i: (i, 0))],
        out_specs=pl.BlockSpec((tm, d), lambda i: (i, 0)),
        scratch_shapes=[pltpu.SMEM((1, tm), jnp.int32), pltpu.SMEM((1, tm), jnp.int32),
                        pltpu.VMEM((tm, d), F32), pltpu.VMEM((tm, d), F32),
                        pltpu.SemaphoreType.DMA, pltpu.SemaphoreType.DMA, pltpu.SemaphoreType.DMA],
        compiler_params=_cparams(("arbitrary",)),
        name="moe_combine",
    )(pos0.reshape(t // tm, 1, tm), pos1.reshape(t // tm, 1, tm), yb, slab)


def moe_routing_tables(slab, tile):
    t = slab.shape[0]
    flat_e = slab[:, :2].astype(jnp.int32).reshape(-1)
    n_assign = flat_e.shape[0]
    onehot = (flat_e[:, None] == jnp.arange(N_EXPERTS)[None, :]).astype(jnp.int32)
    csum = jnp.cumsum(onehot, axis=0)
    counts = csum[-1]
    rank = jnp.take_along_axis(csum, flat_e[:, None], axis=1)[:, 0] - 1
    padded = (counts + tile - 1) // tile * tile
    pad_end = jnp.cumsum(padded)
    pad_start = pad_end - padded
    dest = (pad_start[flat_e] + rank).astype(jnp.int32)
    n_blocks = (n_assign + N_EXPERTS * (tile - 1) + tile - 1) // tile
    buf_tok = jnp.zeros((n_blocks * tile,), jnp.int32).at[dest].set(
        jnp.arange(n_assign, dtype=jnp.int32) // 2, unique_indices=True)
    block_e = jnp.minimum(jnp.searchsorted(pad_end, jnp.arange(n_blocks) * tile, side='right'),
                          N_EXPERTS - 1).astype(jnp.int32)
    dest = dest.reshape(t, 2)
    n_used = (pad_end[-1:] // tile).astype(jnp.int32)
    return buf_tok, block_e, n_used, dest[:, 0], dest[:, 1]


def moe_swiglu(h, router_w, w1, w3, w2, layer):
    slab = moe_route(h, router_w)
    buf_tok, block_e, n_used, pos0, pos1 = moe_routing_tables(slab, MOE_TILE)
    block_e = block_e + layer * N_EXPERTS
    xb = moe_dispatch(h, buf_tok, n_used)
    mid = swiglu_up(xb, w1, w3, block_e, tm=MOE_TILE, tn=512, n_used=n_used)
    yb = gmm(mid, w2, block_e, tm=MOE_TILE, tn=512, out_dtype=F32, n_used=n_used)
    return moe_combine(yb, pos0, pos1, slab)


def _row_tile(k):
    return 1024 if k <= 2048 else 512


def _dense_ids(n_rows, tm, idx):
    return jnp.full((n_rows // tm,), idx, jnp.int32)


def _col_tile(k, tm, n_out, first_col, out_bytes):
    budget = VMEM_LIMIT - 12 * 1024 * 1024
    for tn in (1024, 768, 512, 256):
        if n_out % tn or first_col % tn:
            continue
        working = 2 * k * tn * 4 + k * tn * 2 + 2 * tm * k * 2 + 2 * tm * tn * out_bytes + tm * tn * 4
        if working <= budget:
            return tn
    raise ValueError("no column tile fits")


def _dense(a3, w, idx, *, bias=None, out_dtype=BF16, first_col=0, n_out=None):
    parts = tuple(a3) if isinstance(a3, (tuple, list)) else (a3,)
    b, s, _ = parts[0].shape
    k = w.shape[1]
    m = b * s
    tm = min(_row_tile(k), m)
    n_out = w.shape[2] if n_out is None else n_out
    tn = _col_tile(k, tm, n_out, first_col, jnp.dtype(out_dtype).itemsize)
    out = gmm(tuple(p.reshape(m, p.shape[2]) for p in parts), w, _dense_ids(m, tm, idx), bias, tm=tm, tn=tn,
              out_dtype=out_dtype, col_off=first_col // tn, n_out=n_out)
    return out.reshape(b, s, -1)


def _dense_swiglu(h3, w1, w3, w2, idx):
    b, s, k = h3.shape
    m = b * s
    tm = min(1024, m)
    mid = swiglu_up(h3.reshape(m, k), w1, w3, _dense_ids(m, tm, idx), tm=tm, tn=512)
    tm2 = min(_row_tile(mid.shape[1]), m)
    return gmm(mid, w2, _dense_ids(m, tm2, idx), tm=tm2, tn=512, out_dtype=BF16).reshape(b, s, -1)


def _attention_layer(h, hc, i, w_in, w_out, q_g, k_g, bias, cos2, sin2, ctx_out):
    d = h.shape[-1]
    n_na = d // (2 * HEAD_DIM)
    n_q = n_na
    n_kv = max(1, n_q // 4)
    group = n_q // n_kv
    c_bq, c_ak, c_av = n_na, n_na + n_q, 2 * n_na + n_q
    c_bk, c_bv = 3 * n_na + n_q, 3 * n_na + n_q + n_kv
    q_cols = (n_na + n_q) * HEAD_DIM
    proj = _dense(h, w_in, i)
    if ctx_out:
        projc = _dense(hc, w_in, i)
        shift = 0
    else:
        projc = _dense(hc, w_in, i, first_col=q_cols, n_out=w_in.shape[2] - q_cols)
        shift = n_na + n_q
    qb = head_prep(proj, c_bq, n_q, q_g, cos2, sin2, rope=True)
    kb = head_prep(proj, c_bk, n_kv, k_g, cos2, sin2, rope=True)
    kbc = head_prep(projc, c_bk - shift, n_kv, k_g, cos2, sin2, rope=False)
    o_a = na_attention(proj, projc, n_na, 0, c_ak, c_av, c_ak - shift, c_av - shift, bias)
    o_b = full_attention(qb, 0, n_kv, group, [(kbc, 0), (kb, 0)], [(projc, c_bv - shift), (proj, c_bv)])
    y = _dense((o_a, o_b), w_out, i)
    if not ctx_out:
        return y, None
    qbc = head_prep(projc, c_bq, n_q, q_g, cos2, sin2, rope=False)
    oc_a = full_attention(projc, 0, n_na, 1, [(projc, c_ak)], [(projc, c_av)])
    oc_b = full_attention(qbc, 0, n_kv, group, [(kbc, 0)], [(projc, c_bv)])
    yc = _dense((oc_a, oc_b), w_out, i)
    return y, yc


def _hyena_layer(h, i, p):
    (w_in, b_in, conv_w, conv_b, f_w1, f_b1, f_w2, f_b2, f_w3, freq, skip, w_out, b_out) = p
    n_tok = h.shape[1]
    c = min(HY_CHUNK, n_tok)
    u0 = _dense(h, w_in, i, bias=b_in[i].reshape(1, -1))
    u = short_conv(u0, conv_w[i], conv_b[i])
    fwd_u, fwd_g, inv = dft_matrices(c)
    klag = hyena_lag_filters(n_tok, f_w1[i], f_b1[i], f_w2[i], f_b2[i], f_w3[i], freq[i])
    spectra = filter_spectra(klag, fwd_g, c)
    z = long_conv_gate(u, 0, u, 1, spectra, 0, skip[i], fwd_u, inv, c)
    z = long_conv_gate(z, 0, u, 2, spectra, 1, skip[i], fwd_u, inv, c)
    return _dense(z, w_out, i, bias=b_out[i].reshape(1, -1))


def kernel(x, c, ctx, c_ctx, mod_w, mod_b, norm_g, att_w_in, att_w_out, att_q_norm_g, att_k_norm_g, na_rpb,
           ffn_w1, ffn_w3, ffn_w2, hy_w_in, hy_b_in, hy_conv_w, hy_conv_b, hy_f_w1, hy_f_b1, hy_f_w2, hy_f_b2,
           hy_f_w3, hy_freq, hy_skip, hy_w_out, hy_b_out, moe_router, moe_w1, moe_w3, moe_w2):
    depth = mod_w.shape[0]
    b, s, d = x.shape
    n_ctx = ctx.shape[1]
    d_ff = moe_w1.shape[-1]
    cos2, sin2 = rope_tables(s)
    moe_w1f = moe_w1.reshape(-1, d, d_ff)
    moe_w3f = moe_w3.reshape(-1, d, d_ff)
    moe_w2f = moe_w2.reshape(-1, d_ff, d)
    hy = (hy_w_in, hy_b_in, hy_conv_w, hy_conv_b, hy_f_w1, hy_f_b1, hy_f_w2, hy_f_b2, hy_f_w3, hy_freq,
          hy_skip, hy_w_out, hy_b_out)

    n_mod_rows = 16
    cond = jnp.concatenate([c, c_ctx[None, :], jnp.zeros((n_mod_rows - b - 1, d), F32)], axis=0)

    lat_mods, ctx_mods = [], []
    for l in range(depth):
        mod = gmm(cond, mod_w, _dense_ids(n_mod_rows, n_mod_rows, l), mod_b[l].reshape(1, -1),
                  tm=n_mod_rows, tn=1024, out_dtype=F32, silu_in=True)
        m_lat = mod[:b].reshape(b, 6, 1, d)
        lat_mods.append([m_lat[:, k] for k in range(6)])
        m_ctx = jnp.broadcast_to(mod[b].reshape(1, 6, 1, d), (b, 6, 1, d))
        ctx_mods.append([m_ctx[:, k] for k in range(6)])

    xc = ctx
    h = modulate(x, norm_g[0, 0], lat_mods[0][0], lat_mods[0][1], out_dtype=BF16)
    for l in range(depth):
        even = l % 2 == 0
        i = l // 2
        ctx_out = any(m % 2 == 0 for m in range(l + 1, depth))
        ctx_in = even or ctx_out
        sh1, sc1, g1, sh2, sc2, g2 = lat_mods[l]
        csh1, csc1, cg1, csh2, csc2, cg2 = ctx_mods[l]

        hc = modulate(xc, norm_g[l, 0], csh1, csc1, out_dtype=BF16) if ctx_in else None
        if even:
            bias = na_bias_blocks(na_rpb[i])
            y, yc = _attention_layer(h, hc, i, att_w_in, att_w_out, att_q_norm_g[i], att_k_norm_g[i], bias,
                                     cos2, sin2, ctx_out)
        else:
            y = _hyena_layer(h, i, hy)
            yc = _hyena_layer(hc, i, hy) if ctx_out else None
        ffn_in = BF16 if even else F32
        x, h = residual_modulate(x, y, norm_g[l, 1], g1, norm_g[l, 2], sh2, sc2, out_dtype=ffn_in)
        if ctx_out:
            xc = residual(xc, yc, norm_g[l, 1], cg1)
            hc = modulate(xc, norm_g[l, 2], csh2, csc2, out_dtype=ffn_in)
        y_off = yc_off = 0
        if even:
            y = _dense_swiglu(h, ffn_w1, ffn_w3, ffn_w2, i)
            yc = _dense_swiglu(hc, ffn_w1, ffn_w3, ffn_w2, i) if ctx_out else None
        elif ctx_out and (b * n_ctx) % s == 0:
            tokens = jnp.concatenate([h.reshape(b * s, d), hc.reshape(b * n_ctx, d)], axis=0)
            y_all = moe_swiglu(tokens, moe_router[i], moe_w1f, moe_w3f, moe_w2f, i)
            y = y_all.reshape(-1, s, d)
            yc = y_all.reshape(-1, n_ctx, d)
            yc_off = b * s // n_ctx
        else:
            y = moe_swiglu(h.reshape(b * s, d), moe_router[i], moe_w1f, moe_w3f, moe_w2f, i).reshape(b, s, d)
            yc = None
            if ctx_out:
                yc = moe_swiglu(hc.reshape(b * n_ctx, d), moe_router[i], moe_w1f, moe_w3f, moe_w2f,
                                i).reshape(b, n_ctx, d)
        if l + 1 < depth:
            x, h = residual_modulate(x, y, norm_g[l, 3], g2, norm_g[l + 1, 0], lat_mods[l + 1][0],
                                     lat_mods[l + 1][1], out_dtype=BF16, y_off=y_off)
        else:
            x = residual(x, y, norm_g[l, 3], g2, y_off=y_off)
        if ctx_out:
            xc = residual(xc, yc, norm_g[l, 3], cg2, y_off=yc_off)
    return x
```

```python
import functools
import math

import jax
import jax.numpy as jnp
import numpy as np
from jax import lax
from jax.experimental import pallas as pl
from jax.experimental.pallas import tpu as pltpu

F32 = jnp.float32
BF16 = jnp.bfloat16
HIGHEST = lax.Precision.HIGHEST

GRID_W = 64
HEAD_DIM = 128
NA_WIN_ROWS = 8
NA_WIN_COLS = 16
ROPE_BASE = 10000.0
HY_ORDER = 2
HY_SHORT = 3
HY_EMB_DIM = 33
HY_TARGET = 1e-2
HY_FAST_DECAY = 0.3
HY_SLOW_DECAY = 1.5
HY_MOD_SHIFT = 0.05
N_EXPERTS = 8
NORM_EPS = 1e-6
NEG_INF = -1e30

LANES = 128
V7X_VMEM_BYTES = 64 * 1024 * 1024
VMEM_LIMIT = 56 * 1024 * 1024

ROW_TILE = 512
MOE_TILE = 512
GATHER_UNROLL = 8
HY_CHUNK = 512
NA_QROWS = 4
NA_KROWS = 12


def _cparams(sem):
    return pltpu.CompilerParams(dimension_semantics=sem, vmem_limit_bytes=VMEM_LIMIT)


def _weights_changed(be_ref, i):
    prev = be_ref[jnp.maximum(i - 1, 0)]
    return (i == 0) | (be_ref[i] != prev)


def _next_run_expert(block_e):
    nb = block_e.shape[0]
    idx = jnp.arange(nb, dtype=jnp.int32)
    run_last = jnp.concatenate([block_e[1:] != block_e[:-1], jnp.ones((1,), bool)])
    run_end = lax.cummin(jnp.where(run_last, idx, nb), axis=0, reverse=True)
    nxt = run_end + 1
    return jnp.where(nxt < nb, block_e[jnp.minimum(nxt, nb - 1)], -1).astype(jnp.int32)


def _stream_weights(be_ref, nxt_ref, bufs, sems, tn, col_off):
    j = pl.program_id(0)
    i = pl.program_id(1)
    n_col = pl.num_programs(0)

    def tile_copies(e, jj):
        cols = pl.ds(pl.multiple_of((jj + col_off) * tn, tn), tn)
        return [pltpu.make_async_copy(w.at[e, :, cols], stage, sems.at[n])
                for n, (w, stage, _) in enumerate(bufs)]

    @pl.when((j == 0) & (i == 0))
    def _():
        for cp in tile_copies(be_ref[0], 0):
            cp.start()

    @pl.when(_weights_changed(be_ref, i))
    def _():
        for cp, (_, stage, wbf) in zip(tile_copies(be_ref[i], j), bufs):
            cp.wait()
            wbf[...] = stage[...].astype(BF16)
        next_e = nxt_ref[i]

        @pl.when(next_e >= 0)
        def _():
            for cp in tile_copies(next_e, j):
                cp.start()

        @pl.when((next_e < 0) & (j + 1 < n_col))
        def _():
            for cp in tile_copies(be_ref[0], j + 1):
                cp.start()


def _gmm_kernel(be_ref, nu_ref, nxt_ref, *refs, n_a, silu_in, tn, col_off):
    a_refs = refs[:n_a]
    w_hbm, b_ref, o_ref, stage_ref, wbf_ref, sems = refs[n_a:]
    i = pl.program_id(1)
    used = i < nu_ref[0]
    _stream_weights(be_ref, nxt_ref, [(w_hbm, stage_ref, wbf_ref)], sems, tn, col_off)

    @pl.when(used)
    def _():
        acc = b_ref[...]
        k0 = 0
        for a_ref in a_refs:
            a = a_ref[...]
            if silu_in:
                a = a.astype(F32)
                a = a * jax.nn.sigmoid(a)
            k1 = k0 + a.shape[1]
            acc = acc + jnp.dot(a.astype(BF16), wbf_ref[k0:k1, :], preferred_element_type=F32)
            k0 = k1
        o_ref[...] = acc.astype(o_ref.dtype)

    @pl.when(jnp.logical_not(used))
    def _():
        o_ref[...] = jnp.zeros_like(o_ref)


def _all_used(block_e):
    return jnp.full((1,), block_e.shape[0], jnp.int32)


def gmm(a, w, block_e, bias=None, *, tm, tn, out_dtype, col_off=0, n_out=None, silu_in=False, n_used=None):
    a_parts = tuple(a) if isinstance(a, (tuple, list)) else (a,)
    m = a_parts[0].shape[0]
    k = sum(p.shape[1] for p in a_parts)
    assert k == w.shape[1]
    n_out = w.shape[2] if n_out is None else n_out
    if bias is None:
        bias = jnp.zeros((1, w.shape[2]), F32)
    n_used = _all_used(block_e) if n_used is None else n_used
    grid = (n_out // tn, m // tm)
    a_specs = [pl.BlockSpec((tm, p.shape[1]), lambda j, i, be, nu, nx: (jnp.minimum(i, nu[0] - 1), 0))
               for p in a_parts]
    return pl.pallas_call(
        functools.partial(_gmm_kernel, n_a=len(a_parts), silu_in=silu_in, tn=tn, col_off=col_off),
        out_shape=jax.ShapeDtypeStruct((m, n_out), out_dtype),
        grid_spec=pltpu.PrefetchScalarGridSpec(
            num_scalar_prefetch=3,
            grid=grid,
            in_specs=a_specs + [
                pl.BlockSpec(memory_space=pl.ANY),
                pl.BlockSpec((1, tn), lambda j, i, be, nu, nx: (0, j + col_off)),
            ],
            out_specs=pl.BlockSpec((tm, tn), lambda j, i, be, nu, nx: (i, j)),
            scratch_shapes=[pltpu.VMEM((k, tn), F32), pltpu.VMEM((k, tn), BF16),
                            pltpu.SemaphoreType.DMA((1,))],
        ),
        compiler_params=_cparams(("arbitrary", "arbitrary")),
        name="gmm",
    )(block_e, n_used, _next_run_expert(block_e), *a_parts, w, bias)


def _up_kernel(be_ref, nu_ref, nxt_ref, a_ref, w1_hbm, w3_hbm, o_ref, stage1_ref, stage3_ref, w1bf_ref,
               w3bf_ref, sems, *, tn):
    i = pl.program_id(1)
    used = i < nu_ref[0]
    _stream_weights(be_ref, nxt_ref, [(w1_hbm, stage1_ref, w1bf_ref), (w3_hbm, stage3_ref, w3bf_ref)],
                    sems, tn, 0)

    @pl.when(used)
    def _():
        a = a_ref[...]
        g = jnp.dot(a, w1bf_ref[...], preferred_element_type=F32)
        u = jnp.dot(a, w3bf_ref[...], preferred_element_type=F32)
        o_ref[...] = (g * jax.nn.sigmoid(g) * u).astype(o_ref.dtype)

    @pl.when(jnp.logical_not(used))
    def _():
        o_ref[...] = jnp.zeros_like(o_ref)


def swiglu_up(a, w1, w3, block_e, *, tm, tn, n_used=None):
    m, k = a.shape
    n = w1.shape[2]
    n_used = _all_used(block_e) if n_used is None else n_used
    grid = (n // tn, m // tm)
    return pl.pallas_call(
        functools.partial(_up_kernel, tn=tn),
        out_shape=jax.ShapeDtypeStruct((m, n), BF16),
        grid_spec=pltpu.PrefetchScalarGridSpec(
            num_scalar_prefetch=3,
            grid=grid,
            in_specs=[
                pl.BlockSpec((tm, k), lambda j, i, be, nu, nx: (jnp.minimum(i, nu[0] - 1), 0)),
                pl.BlockSpec(memory_space=pl.ANY),
                pl.BlockSpec(memory_space=pl.ANY),
            ],
            out_specs=pl.BlockSpec((tm, tn), lambda j, i, be, nu, nx: (i, j)),
            scratch_shapes=[pltpu.VMEM((k, tn), F32), pltpu.VMEM((k, tn), F32),
                            pltpu.VMEM((k, tn), BF16), pltpu.VMEM((k, tn), BF16),
                            pltpu.SemaphoreType.DMA((2,))],
        ),
        compiler_params=_cparams(("arbitrary", "arbitrary")),
        name="swiglu_up",
    )(block_e, n_used, _next_run_expert(block_e), a, w1, w3)


def _rms(x, g):
    ms = jnp.mean(x * x, axis=-1, keepdims=True)
    return x * lax.rsqrt(ms + NORM_EPS) * g


def _mod_kernel(x_ref, g_ref, sh_ref, sc_ref, o_ref):
    y = _rms(x_ref[0], g_ref[...])
    o_ref[0] = (y * (1.0 + sc_ref[0]) + sh_ref[0]).astype(o_ref.dtype)


def modulate(x, g, shift, scale, *, out_dtype, tr=ROW_TILE):
    b, s, d = x.shape
    tr = min(tr, s)
    row = pl.BlockSpec((1, tr, d), lambda bi, si: (bi, si, 0))
    vec = pl.BlockSpec((1, 1, d), lambda bi, si: (bi, 0, 0))
    return pl.pallas_call(
        _mod_kernel,
        out_shape=jax.ShapeDtypeStruct((b, s, d), out_dtype),
        grid=(b, s // tr),
        in_specs=[row, pl.BlockSpec((1, d), lambda bi, si: (0, 0)), vec, vec],
        out_specs=row,
        compiler_params=_cparams(("parallel", "parallel")),
        name="modulate",
    )(x, g.reshape(1, d), shift, scale)


def _resid_kernel(x_ref, y_ref, g_ref, gate_ref, o_ref):
    y = _rms(y_ref[0].astype(F32), g_ref[...])
    o_ref[0] = x_ref[0] + gate_ref[0] * y


def residual(x, y, g, gate, *, tr=ROW_TILE, y_off=0):
    b, s, d = x.shape
    tr = min(tr, s)
    row = pl.BlockSpec((1, tr, d), lambda bi, si: (bi, si, 0))
    y_row = pl.BlockSpec((1, tr, d), lambda bi, si: (bi + y_off, si, 0))
    vec = pl.BlockSpec((1, 1, d), lambda bi, si: (bi, 0, 0))
    return pl.pallas_call(
        _resid_kernel,
        out_shape=jax.ShapeDtypeStruct((b, s, d), F32),
        grid=(b, s // tr),
        in_specs=[row, y_row, pl.BlockSpec((1, d), lambda bi, si: (0, 0)), vec],
        out_specs=row,
        compiler_params=_cparams(("parallel", "parallel")),
        name="residual",
    )(x, y, g.reshape(1, d), gate)


def _resid_mod_kernel(x_ref, y_ref, gy_ref, gate_ref, gh_ref, sh_ref, sc_ref, xo_ref, ho_ref):
    x = x_ref[0] + gate_ref[0] * _rms(y_ref[0].astype(F32), gy_ref[...])
    xo_ref[0] = x
    ho_ref[0] = (_rms(x, gh_ref[...]) * (1.0 + sc_ref[0]) + sh_ref[0]).astype(ho_ref.dtype)


def residual_modulate(x, y, g_y, gate, g_h, shift, scale, *, out_dtype, tr=ROW_TILE, y_off=0):
    b, s, d = x.shape
    tr = min(tr, s)
    row = pl.BlockSpec((1, tr, d), lambda bi, si: (bi, si, 0))
    y_row = pl.BlockSpec((1, tr, d), lambda bi, si: (bi + y_off, si, 0))
    vec = pl.BlockSpec((1, 1, d), lambda bi, si: (bi, 0, 0))
    gain = pl.BlockSpec((1, d), lambda bi, si: (0, 0))
    return pl.pallas_call(
        _resid_mod_kernel,
        out_shape=(jax.ShapeDtypeStruct((b, s, d), F32), jax.ShapeDtypeStruct((b, s, d), out_dtype)),
        grid=(b, s // tr),
        in_specs=[row, y_row, gain, vec, gain, vec, vec],
        out_specs=(row, row),
        compiler_params=_cparams(("parallel", "parallel")),
        name="residual_modulate",
    )(x, y, g_y.reshape(1, d), gate, g_h.reshape(1, d), shift, scale)


def _prep_kernel(x_ref, g_ref, cos_ref, sin_ref, o_ref, *, rope, n_heads):
    for h in range(n_heads):
        cols = slice(h * HEAD_DIM, (h + 1) * HEAD_DIM)
        y = _rms(x_ref[0, :, cols].astype(F32), g_ref[...])
        if rope:
            rot = pltpu.roll(y, HEAD_DIM // 2, axis=1)
            y = y * cos_ref[...] + rot * sin_ref[...]
        o_ref[0, :, cols] = y.astype(o_ref.dtype)


def head_prep(src, col0, n_heads, g, cos2, sin2, *, rope):
    b, s, _ = src.shape
    ts = min(s, 512)
    width = n_heads * HEAD_DIM
    assert col0 % n_heads == 0
    return pl.pallas_call(
        functools.partial(_prep_kernel, rope=rope, n_heads=n_heads),
        out_shape=jax.ShapeDtypeStruct((b, s, width), BF16),
        grid=(b, s // ts),
        in_specs=[
            pl.BlockSpec((1, ts, width), lambda bi, si: (bi, si, col0 // n_heads)),
            pl.BlockSpec((1, HEAD_DIM), lambda bi, si: (0, 0)),
            pl.BlockSpec((ts, HEAD_DIM), lambda bi, si: (si, 0)),
            pl.BlockSpec((ts, HEAD_DIM), lambda bi, si: (si, 0)),
        ],
        out_specs=pl.BlockSpec((1, ts, width), lambda bi, si: (bi, si, 0)),
        compiler_params=_cparams(("parallel", "parallel")),
        name="head_prep",
    )(src, g.reshape(1, HEAD_DIM), cos2[:s], sin2[:s])


def _nt_dot(q, k):
    return lax.dot_general(q, k, (((1,), (1,)), ((), ())), preferred_element_type=F32)


def _with_ones(v):
    return jnp.concatenate([v, jnp.ones(v.shape, v.dtype)], axis=1)


def _softmax_pv(scores, values_ext, out_dtype):
    m = scores[0].max(axis=-1, keepdims=True)
    for s in scores[1:]:
        m = jnp.maximum(m, s.max(axis=-1, keepdims=True))
    ps = [jnp.exp((s - m).astype(BF16)) for s in scores]
    acc = jnp.dot(ps[0], values_ext[0], preferred_element_type=F32)
    for p, v in zip(ps[1:], values_ext[1:]):
        acc = acc + jnp.dot(p, v, preferred_element_type=F32)
    return (acc[:, :HEAD_DIM] / acc[:, HEAD_DIM:HEAD_DIM + 1]).astype(out_dtype)


def _scaled(q):
    return (q.astype(F32) * (HEAD_DIM ** -0.5)).astype(BF16)


def _attn_kernel(*refs, group, n_parts):
    q_ref = refs[0]
    k_refs = refs[1:1 + n_parts]
    v_refs = refs[1 + n_parts:1 + 2 * n_parts]
    o_ref = refs[1 + 2 * n_parts]
    ks = [r[0] for r in k_refs]
    vs = [_with_ones(r[0]) for r in v_refs]
    for g in range(group):
        cols = slice(g * HEAD_DIM, (g + 1) * HEAD_DIM)
        q = _scaled(q_ref[0, :, cols])
        scores = [_nt_dot(q, k) for k in ks]
        o_ref[0, :, cols] = _softmax_pv(scores, vs, o_ref.dtype)


def full_attention(q_src, q_col0, n_kv, group, k_parts, v_parts, *, tq=512):
    b, sq, _ = q_src.shape
    tq = min(tq, sq)
    gw = group * HEAD_DIM
    assert (q_col0 * HEAD_DIM) % gw == 0
    q_blk0 = q_col0 * HEAD_DIM // gw
    in_specs = [pl.BlockSpec((1, tq, gw), lambda bi, h, qi: (bi, qi, q_blk0 + h))]
    args = [q_src]
    for arr, c0 in list(k_parts) + list(v_parts):
        in_specs.append(pl.BlockSpec((1, arr.shape[1], HEAD_DIM),
                                     functools.partial(lambda bi, h, qi, c0: (bi, 0, c0 + h), c0=c0)))
        args.append(arr)
    return pl.pallas_call(
        functools.partial(_attn_kernel, group=group, n_parts=len(k_parts)),
        out_shape=jax.ShapeDtypeStruct((b, sq, n_kv * gw), BF16),
        grid=(b, n_kv, sq // tq),
        in_specs=in_specs,
        out_specs=pl.BlockSpec((1, tq, gw), lambda bi, h, qi: (bi, qi, h)),
        compiler_params=_cparams(("parallel", "parallel", "parallel")),
        name="full_attention",
    )(*args)


NA_MASKED = 2 * NA_WIN_ROWS - 1


def _na_bias(bias_ref, r0, start, rows):
    lane = lax.broadcasted_iota(jnp.int32, (GRID_W, 2 * GRID_W), 1)
    bands = []
    for i in range(NA_QROWS):
        r = r0 + i
        rs = min(max(r - NA_WIN_ROWS // 2, 0), rows - NA_WIN_ROWS)

        def block_id(j):
            kr = start + j
            return kr - r + NA_WIN_ROWS - 1 if rs <= kr < rs + NA_WIN_ROWS else NA_MASKED

        tiles = []
        for m in range(NA_KROWS // 2):
            d0, d1 = block_id(2 * m), block_id(2 * m + 1)
            tile = bias_ref[0, d0]
            if d1 != d0:
                tile = jnp.where(lane < GRID_W, tile, bias_ref[0, d1])
            tiles.append(tile)
        bands.append(jnp.concatenate(tiles, axis=1))
    return jnp.concatenate(bands, axis=0)


def _na_kernel(q_ref, k_ref, v_ref, kc_ref, vc_ref, bias_ref, o_ref, *, rows):
    n_steps = rows // NA_QROWS
    kc = kc_ref[0]
    vc = _with_ones(vc_ref[0])
    for step in range(n_steps):
        r0 = step * NA_QROWS
        start = min(max(r0 - NA_WIN_ROWS // 2, 0), rows - NA_KROWS)
        qs = slice(r0 * GRID_W, (r0 + NA_QROWS) * GRID_W)
        kslice = slice(start * GRID_W, (start + NA_KROWS) * GRID_W)
        q = _scaled(q_ref[0, qs, :])
        s_lat = _nt_dot(q, k_ref[0, kslice, :]) + _na_bias(bias_ref, r0, start, rows)
        s_ctx = _nt_dot(q, kc)
        o_ref[0, qs, :] = _softmax_pv([s_lat, s_ctx], [_with_ones(v_ref[0, kslice, :]), vc], o_ref.dtype)


def na_bias_blocks(rpb):
    n_dc = 2 * NA_WIN_COLS - 1
    col = np.arange(GRID_W)
    col_start = np.clip(col - NA_WIN_COLS // 2, 0, GRID_W - NA_WIN_COLS)
    col_ok = (col[None, :] >= col_start[:, None]) & (col[None, :] < col_start[:, None] + NA_WIN_COLS)
    dc = np.clip(col[None, :] - col[:, None] + NA_WIN_COLS - 1, 0, n_dc - 1)
    pick_dc = (dc[:, :, None] == np.arange(n_dc)).astype(np.float32)
    blocks = jnp.einsum('hde,qke->hdqk', rpb.astype(F32), pick_dc, precision=HIGHEST)
    blocks = jnp.where(col_ok[None, None], blocks, NEG_INF)
    masked = jnp.full((rpb.shape[0], 1, GRID_W, GRID_W), NEG_INF, F32)
    blocks = jnp.concatenate([blocks, masked], axis=1)
    return jnp.concatenate([blocks, blocks], axis=-1)


def na_attention(proj, projc, n_heads, q_col0, k_col0, v_col0, kc_col0, vc_col0, bias):
    b, s, _ = proj.shape
    n_ctx = projc.shape[1]
    rows = s // GRID_W
    assert rows % NA_QROWS == 0 and NA_KROWS % 2 == 0 and 2 * GRID_W == LANES
    lat = lambda c0: pl.BlockSpec((1, s, HEAD_DIM), lambda h, bi: (bi, 0, c0 + h))
    ctx = lambda c0: pl.BlockSpec((1, n_ctx, HEAD_DIM), lambda h, bi: (bi, 0, c0 + h))
    return pl.pallas_call(
        functools.partial(_na_kernel, rows=rows),
        out_shape=jax.ShapeDtypeStruct((b, s, n_heads * HEAD_DIM), BF16),
        grid=(n_heads, b),
        in_specs=[lat(q_col0), lat(k_col0), lat(v_col0), ctx(kc_col0), ctx(vc_col0),
                  pl.BlockSpec((1,) + bias.shape[1:], lambda h, bi: (h, 0, 0, 0))],
        out_specs=pl.BlockSpec((1, s, HEAD_DIM), lambda h, bi: (bi, 0, h)),
        compiler_params=_cparams(("parallel", "parallel")),
        name="na_attention",
    )(proj, proj, proj, projc, projc, bias)


def rope_tables(n_tok):
    t = jnp.arange(n_tok)
    row = (t // GRID_W).astype(F32)
    col = (t % GRID_W).astype(F32)
    n_freq = HEAD_DIM // 4
    inv_freq = ROPE_BASE ** (-jnp.arange(n_freq, dtype=F32) / n_freq)
    ang = jnp.concatenate([row[:, None] * inv_freq, col[:, None] * inv_freq], axis=-1)
    cos, sin = jnp.cos(ang), jnp.sin(ang)
    return jnp.concatenate([cos, cos], axis=-1), jnp.concatenate([-sin, sin], axis=-1)


def _shortconv_kernel(u_ref, w_ref, b_ref, o_ref):
    u = u_ref[0].astype(F32)
    n = u.shape[0]
    row = lax.broadcasted_iota(jnp.int32, u.shape, 0)
    prev = jnp.where(row == 0, 0.0, pltpu.roll(u, 1, axis=0))
    nxt = jnp.where(row == n - 1, 0.0, pltpu.roll(u, n - 1, axis=0))
    o = prev * w_ref[0:1, :] + u * w_ref[1:2, :] + nxt * w_ref[2:3, :] + b_ref[...]
    o_ref[0] = o.astype(o_ref.dtype)


def short_conv(u, w, bias, *, tc=256):
    b, n, c = u.shape
    return pl.pallas_call(
        _shortconv_kernel,
        out_shape=jax.ShapeDtypeStruct((b, n, c), BF16),
        grid=(b, c // tc),
        in_specs=[pl.BlockSpec((1, n, tc), lambda bi, j: (bi, 0, j)),
                  pl.BlockSpec((HY_SHORT, tc), lambda bi, j: (0, j)),
                  pl.BlockSpec((1, tc), lambda bi, j: (0, j))],
        out_specs=pl.BlockSpec((1, n, tc), lambda bi, j: (bi, 0, j)),
        compiler_params=_cparams(("parallel", "parallel")),
        name="short_conv",
    )(u, w, bias.reshape(1, c))


def _filter_mlp_kernel(z_ref, w1_ref, b1_ref, w2_ref, b2_ref, fr_ref, o_ref):
    fr = fr_ref[...]
    a = jnp.sin(fr * (jnp.dot(z_ref[...], w1_ref[...], precision=HIGHEST, preferred_element_type=F32) + b1_ref[...]))
    o_ref[...] = jnp.sin(fr * (jnp.dot(a, w2_ref[...], precision=HIGHEST, preferred_element_type=F32) + b2_ref[...]))


def _split_bf16(x):
    hi = x.astype(BF16)
    return hi, (x - hi.astype(F32)).astype(BF16)


def _dot3(a, w):
    a_hi, a_lo = _split_bf16(a)
    w_hi, w_lo = _split_bf16(w)
    return (jnp.dot(a_hi, w_hi, preferred_element_type=F32) + jnp.dot(a_lo, w_hi, preferred_element_type=F32)
            + jnp.dot(a_hi, w_lo, preferred_element_type=F32))


def _filter_kernel(a_ref, t_ref, w3f_ref, w3b_ref, dl_ref, o_ref, *, n_tok):
    a = a_ref[...]
    f_fwd = _dot3(a, w3f_ref[...])
    f_bwd = _dot3(a, w3b_ref[...])
    window = jnp.exp(-t_ref[...] * dl_ref[...]) + HY_MOD_SHIFT
    lag_row = lax.broadcasted_iota(jnp.int32, f_fwd.shape, 0)
    filt = jnp.where(lag_row >= n_tok, f_fwd, f_bwd) * window
    o_ref[0] = jnp.where(lag_row == 0, 0.0, filt)


def hyena_lag_filters(n_tok, f_w1, f_b1, f_w2, f_b2, f_w3, freq, *, tc=256):
    d = f_w3.shape[1] // (2 * HY_ORDER)
    hidden = f_w1.shape[1]
    bands = (HY_EMB_DIM - 1) // 2
    pos = jnp.abs(jnp.arange(2 * n_tok) - n_tok)
    pos = jnp.minimum(pos, n_tok - 1)
    t = jnp.linspace(0.0, 1.0, n_tok, dtype=F32)[pos][:, None]
    w = (2.0 * math.pi * jnp.arange(n_tok, dtype=F32) / n_tok)[pos][:, None]
    f = jnp.linspace(1e-4, bands - 1, bands, dtype=F32)[None, :]
    z = jnp.concatenate([t, jnp.cos(f * w), -jnp.sin(f * w)], axis=-1)
    z = jnp.pad(z, ((0, 0), (0, LANES - HY_EMB_DIM)))
    w1 = jnp.pad(f_w1.astype(F32), ((0, LANES - HY_EMB_DIM), (0, 0)))
    max_decay = math.log(HY_TARGET) / HY_FAST_DECAY
    min_decay = math.log(HY_TARGET) / HY_SLOW_DECAY
    deltas = jnp.abs(jnp.linspace(min_decay, max_decay, d, dtype=F32)).reshape(1, d)
    nj = d // tc
    n_lag = 2 * n_tok
    one = lambda shape: pl.BlockSpec(shape, lambda i: (0,) * len(shape))
    feats = pl.pallas_call(
        _filter_mlp_kernel,
        out_shape=jax.ShapeDtypeStruct((n_lag, hidden), F32),
        grid=(1,),
        in_specs=[one((n_lag, LANES)), one((LANES, hidden)), one((1, hidden)), one((hidden, hidden)),
                  one((1, hidden)), one((1, hidden))],
        out_specs=one((n_lag, hidden)),
        compiler_params=_cparams(("arbitrary",)),
        name="hyena_filter_mlp",
    )(z, w1, f_b1.reshape(1, hidden).astype(F32), f_w2.astype(F32), f_b2.reshape(1, hidden).astype(F32),
      freq.reshape(1, hidden).astype(F32))
    full = lambda shape: pl.BlockSpec(shape, lambda o, j: (0,) * len(shape))
    return pl.pallas_call(
        functools.partial(_filter_kernel, n_tok=n_tok),
        out_shape=jax.ShapeDtypeStruct((HY_ORDER, n_lag, d), F32),
        grid=(HY_ORDER, nj),
        in_specs=[full((n_lag, hidden)), full((n_lag, 1)),
                  pl.BlockSpec((hidden, tc), lambda o, j: (0, (2 * o) * nj + j)),
                  pl.BlockSpec((hidden, tc), lambda o, j: (0, (2 * o + 1) * nj + j)),
                  pl.BlockSpec((1, tc), lambda o, j: (0, j))],
        out_specs=pl.BlockSpec((1, n_lag, tc), lambda o, j: (o, 0, j)),
        compiler_params=_cparams(("parallel", "parallel")),
        name="hyena_filters",
    )(feats, t, f_w3.astype(F32), f_w3.astype(F32), deltas)


def _dft_angles(f, n, c):
    k = jnp.mod((2 * f + 1) * n, 4 * c).astype(F32)
    return k * (2.0 * math.pi / (4 * c))


def dft_matrices(c):
    f = jnp.arange(c)[:, None]
    ang_u = _dft_angles(f, jnp.arange(c)[None, :], c)
    fwd_u = jnp.concatenate([jnp.cos(ang_u), -jnp.sin(ang_u)], axis=0)
    ang_g = _dft_angles(f, jnp.arange(-c, c)[None, :], c)
    fwd_g = jnp.concatenate([jnp.cos(ang_g), -jnp.sin(ang_g)], axis=0)
    inv = jnp.concatenate([jnp.cos(ang_u).T, -jnp.sin(ang_u).T], axis=1) / c
    return fwd_u.astype(BF16), fwd_g.astype(BF16), inv.astype(BF16)


def _spectra_kernel(fa_ref, fb_ref, ka_ref, kb_ref, o_ref):
    acc = jnp.dot(fa_ref[...], ka_ref[0].astype(BF16), preferred_element_type=F32)
    acc = acc + jnp.dot(fb_ref[...], kb_ref[0].astype(BF16), preferred_element_type=F32)
    o_ref[0, 0] = acc.astype(o_ref.dtype)


def filter_spectra(klag, fwd_g, c, *, tc=1024):
    n_ord, two_l, d = klag.shape
    n_delta = two_l // c - 1
    fa, fb = fwd_g[:, :c], fwd_g[:, c:]
    return pl.pallas_call(
        _spectra_kernel,
        out_shape=jax.ShapeDtypeStruct((n_ord, n_delta, 2 * c, d), BF16),
        grid=(n_ord, n_delta, d // tc),
        in_specs=[pl.BlockSpec((2 * c, c), lambda o, dl, j: (0, 0)),
                  pl.BlockSpec((2 * c, c), lambda o, dl, j: (0, 0)),
                  pl.BlockSpec((1, c, tc), lambda o, dl, j: (o, dl, j)),
                  pl.BlockSpec((1, c, tc), lambda o, dl, j: (o, dl + 1, j))],
        out_specs=pl.BlockSpec((1, 1, 2 * c, tc), lambda o, dl, j: (o, dl, 0, j)),
        compiler_params=_cparams(("parallel", "parallel", "parallel")),
        name="filter_spectra",
    )(fa, fb, klag, klag)


def _longconv_kernel(u_ref, gate_ref, sk_ref, skip_ref, fu_ref, inv_ref, o_ref, spec_ref, y_ref, *, c, n_ch, order):
    for j in range(n_ch):
        spec_ref[j] = jnp.dot(fu_ref[...], u_ref[0, j * c:(j + 1) * c, :],
                              preferred_element_type=F32).astype(BF16)
    skip = skip_ref[order:order + 1, :]
    for i in range(n_ch):
        top = None
        bot = None
        for j in range(n_ch):
            dl = i - j + n_ch - 1
            xr = spec_ref[j, :c, :]
            xi = spec_ref[j, c:, :]
            gr = sk_ref[0, dl, :c, :]
            gi = sk_ref[0, dl, c:, :]
            t = xr * gr - xi * gi
            b = xr * gi + xi * gr
            top = t if top is None else top + t
            bot = b if bot is None else bot + b
        y_ref[i, :c, :] = top
        y_ref[i, c:, :] = bot
        y = jnp.dot(inv_ref[...], y_ref[i], preferred_element_type=F32)
        rows = slice(i * c, (i + 1) * c)
        u = u_ref[0, rows, :].astype(F32)
        o_ref[0, rows, :] = (gate_ref[0, rows, :].astype(F32) * (y + u * skip)).astype(o_ref.dtype)


def long_conv_gate(u_src, u_col0, gate_src, gate_col0, spectra, order, skip, fwd_u, inv, c, *, tc=512):
    b, n, _ = u_src.shape
    d = spectra.shape[3]
    n_ch = n // c
    n_delta = spectra.shape[1]
    nj = d // tc
    return pl.pallas_call(
        functools.partial(_longconv_kernel, c=c, n_ch=n_ch, order=order),
        out_shape=jax.ShapeDtypeStruct((b, n, d), BF16),
        grid=(nj, b),
        in_specs=[pl.BlockSpec((1, n, tc), lambda j, bi: (bi, 0, u_col0 * nj + j)),
                  pl.BlockSpec((1, n, tc), lambda j, bi: (bi, 0, gate_col0 * nj + j)),
                  pl.BlockSpec((1, n_delta, 2 * c, tc), lambda j, bi: (order, 0, 0, j)),
                  pl.BlockSpec((HY_ORDER, tc), lambda j, bi: (0, j)),
                  pl.BlockSpec((2 * c, c), lambda j, bi: (0, 0)),
                  pl.BlockSpec((c, 2 * c), lambda j, bi: (0, 0))],
        out_specs=pl.BlockSpec((1, n, tc), lambda j, bi: (bi, 0, j)),
        scratch_shapes=[pltpu.VMEM((n_ch, 2 * c, tc), BF16), pltpu.VMEM((n_ch, 2 * c, tc), BF16)],
        compiler_params=_cparams(("parallel", "parallel")),
        name="long_conv_gate",
    )(u_src, gate_src, spectra, skip.astype(F32), fwd_u, inv)


def _router_kernel(h_ref, w_ref, o_ref):
    logits = _dot3(h_ref[...], w_ref[...])
    lane = lax.broadcasted_iota(jnp.int32, logits.shape, 1).astype(F32)
    lg = jnp.where(lane < N_EXPERTS, logits, -jnp.inf)
    m1 = lg.max(axis=-1, keepdims=True)
    e1 = jnp.where(lg == m1, lane, float(LANES)).min(axis=-1, keepdims=True)
    lg2 = jnp.where(lane == e1, -jnp.inf, lg)
    m2 = lg2.max(axis=-1, keepdims=True)
    e2 = jnp.where(lg2 == m2, lane, float(LANES)).min(axis=-1, keepdims=True)
    t = jnp.exp(m2 - m1)
    p1 = 1.0 / (1.0 + t)
    p2 = t / (1.0 + t)
    o_ref[...] = jnp.where(lane == 0, e1, jnp.where(lane == 1, e2, jnp.where(lane == 2, p1,
                           jnp.where(lane == 3, p2, 0.0))))


def moe_route(h, router_w, *, tm=512):
    t, d = h.shape
    w = jnp.pad(router_w.astype(F32), ((0, 0), (0, LANES - N_EXPERTS)))
    return pl.pallas_call(
        _router_kernel,
        out_shape=jax.ShapeDtypeStruct((t, LANES), F32),
        grid=(t // tm,),
        in_specs=[pl.BlockSpec((tm, d), lambda i: (i, 0)), pl.BlockSpec((d, LANES), lambda i: (0, 0))],
        out_specs=pl.BlockSpec((tm, LANES), lambda i: (i, 0)),
        compiler_params=_cparams(("parallel",)),
        name="moe_router",
    )(h, w)


def _row_copy(src_ref, dst_ref, sem, src_row, dst_row):
    return pltpu.make_async_copy(src_ref.at[pl.ds(src_row, 1)], dst_ref.at[pl.ds(dst_row, 1)], sem)


def _load_indices(idx_vmem_ref, idx_smem, isem):
    cp = pltpu.make_async_copy(idx_vmem_ref, idx_smem, isem)
    cp.start()
    cp.wait()


def _issue_rows(idx_smem, idx_row, src_ref, buf_ref, sem, n):
    def issue(blk, carry):
        for u in range(GATHER_UNROLL):
            r = blk * GATHER_UNROLL + u
            _row_copy(src_ref, buf_ref, sem, idx_smem[idx_row, r], r).start()
        return carry

    lax.fori_loop(0, n // GATHER_UNROLL, issue, 0)


def _wait_rows(src_ref, buf_ref, sem, n):
    pltpu.make_async_copy(src_ref.at[pl.ds(0, n)], buf_ref, sem).wait()


def _gather_one_step_ahead(step, n_active, idx_refs, idx_next_refs, src_ref, idx_smem, buf_ref, sems, isem, tm):
    n_idx = len(idx_refs)
    slot = lax.rem(step, 2)

    def start(refs, s):
        for q, idx_ref in enumerate(refs):
            row = s * n_idx + q
            _load_indices(idx_ref.at[0], idx_smem.at[pl.ds(row, 1)], isem)
            _issue_rows(idx_smem, row, src_ref, buf_ref.at[s, q], sems.at[s, q], tm)

    @pl.when(step == 0)
    def _():
        start(idx_refs, 0)

    @pl.when(step + 1 < n_active)
    def _():
        start(idx_next_refs, 1 - slot)

    for q in range(n_idx):
        _wait_rows(src_ref, buf_ref.at[slot, q], sems.at[slot, q], tm)
    return slot


def _dispatch_kernel(nu_ref, idx_ref, idx_next_ref, src_ref, o_ref, idx_smem, buf_ref, sems, isem, *, tm):
    step = pl.program_id(0)
    used = step < nu_ref[0]

    @pl.when(used)
    def _():
        slot = _gather_one_step_ahead(step, nu_ref[0], [idx_ref], [idx_next_ref], src_ref, idx_smem, buf_ref,
                                      sems, isem, tm)
        o_ref[...] = buf_ref[slot, 0].astype(o_ref.dtype)

    @pl.when(jnp.logical_not(used))
    def _():
        o_ref[...] = jnp.zeros_like(o_ref)


def moe_dispatch(h, buf_tok, n_used, *, tm=MOE_TILE):
    _, d = h.shape
    n_rows = buf_tok.shape[0]
    n_blocks = n_rows // tm
    idx = buf_tok.reshape(n_blocks, 1, tm)
    return pl.pallas_call(
        functools.partial(_dispatch_kernel, tm=tm),
        out_shape=jax.ShapeDtypeStruct((n_rows, d), BF16),
        grid_spec=pltpu.PrefetchScalarGridSpec(
            num_scalar_prefetch=1,
            grid=(n_blocks,),
            in_specs=[pl.BlockSpec((1, 1, tm), lambda i, nu: (i, 0, 0)),
                      pl.BlockSpec((1, 1, tm), lambda i, nu: (jnp.minimum(i + 1, n_blocks - 1), 0, 0)),
                      pl.BlockSpec(memory_space=pl.ANY)],
            out_specs=pl.BlockSpec((tm, d), lambda i, nu: (i, 0)),
            scratch_shapes=[pltpu.SMEM((2, tm), jnp.int32), pltpu.VMEM((2, 1, tm, d), F32),
                            pltpu.SemaphoreType.DMA((2, 1)), pltpu.SemaphoreType.DMA],
        ),
        compiler_params=_cparams(("arbitrary",)),
        name="moe_dispatch",
    )(n_used, idx, idx, h)


def _combine_kernel(i0_ref, i1_ref, i0_next_ref, i1_next_ref, yb_ref, slab_ref, o_ref, idx_smem, buf_ref,
                    sems, isem, *, tm):
    slot = _gather_one_step_ahead(pl.program_id(0), pl.num_programs(0), [i0_ref, i1_ref],
                                  [i0_next_ref, i1_next_ref], yb_ref, idx_smem, buf_ref, sems, isem, tm)
    o_ref[...] = buf_ref[slot, 0] * slab_ref[:, 2:3] + buf_ref[slot, 1] * slab_ref[:, 3:4]


def moe_combine(yb, pos0, pos1, slab, *, tm=512):
    t = pos0.shape[0]
    d = yb.shape[1]
    n_blocks = t // tm
    idx_spec = pl.BlockSpec((1, 1, tm), lambda i: (i, 0, 0))
    idx_next_spec = pl.BlockSpec((1, 1, tm), lambda i: (jnp.minimum(i + 1, n_blocks - 1), 0, 0))
    idx0 = pos0.reshape(n_blocks, 1, tm)
    idx1 = pos1.reshape(n_blocks, 1, tm)
    return pl.pallas_call(
        functools.partial(_combine_kernel, tm=tm),
        out_shape=jax.ShapeDtypeStruct((t, d), F32),
        grid=(n_blocks,),
        in_specs=[idx_spec, idx_spec, idx_next_spec, idx_next_spec, pl.BlockSpec(memory_space=pl.ANY),
                  pl.BlockSpec((tm, LANES), lambda i: (i, 0))],
        out_specs=pl.BlockSpec((tm, d), lambda i: (i, 0)),
        scratch_shapes=[pltpu.SMEM((4, tm), jnp.int32), pltpu.VMEM((2, 2, tm, d), F32),
                        pltpu.SemaphoreType.DMA((2, 2)), pltpu.SemaphoreType.DMA],
        compiler_params=_cparams(("arbitrary",)),
        name="moe_combine",
    )(idx0, idx1, idx0, idx1, yb, slab)


def moe_routing_tables(slab, tile):
    t = slab.shape[0]
    flat_e = slab[:, :2].astype(jnp.int32).reshape(-1)
    n_assign = flat_e.shape[0]
    onehot = (flat_e[:, None] == jnp.arange(N_EXPERTS)[None, :]).astype(jnp.int32)
    csum = jnp.cumsum(onehot, axis=0)
    counts = csum[-1]
    rank = jnp.take_along_axis(csum, flat_e[:, None], axis=1)[:, 0] - 1
    padded = (counts + tile - 1) // tile * tile
    pad_end = jnp.cumsum(padded)
    pad_start = pad_end - padded
    dest = (pad_start[flat_e] + rank).astype(jnp.int32)
    n_blocks = (n_assign + N_EXPERTS * (tile - 1) + tile - 1) // tile
    buf_tok = jnp.zeros((n_blocks * tile,), jnp.int32).at[dest].set(
        jnp.arange(n_assign, dtype=jnp.int32) // 2, unique_indices=True)
    block_e = jnp.minimum(jnp.searchsorted(pad_end, jnp.arange(n_blocks) * tile, side='right'),
                          N_EXPERTS - 1).astype(jnp.int32)
    dest = dest.reshape(t, 2)
    n_used = (pad_end[-1:] // tile).astype(jnp.int32)
    return buf_tok, block_e, n_used, dest[:, 0], dest[:, 1]


def moe_swiglu(h, router_w, w1, w3, w2, layer):
    slab = moe_route(h, router_w)
    buf_tok, block_e, n_used, pos0, pos1 = moe_routing_tables(slab, MOE_TILE)
    block_e = block_e + layer * N_EXPERTS
    xb = moe_dispatch(h, buf_tok, n_used)
    mid = swiglu_up(xb, w1, w3, block_e, tm=MOE_TILE, tn=512, n_used=n_used)
    yb = gmm(mid, w2, block_e, tm=MOE_TILE, tn=512, out_dtype=F32, n_used=n_used)
    return moe_combine(yb, pos0, pos1, slab)


def _row_tile(k):
    return 1024 if k <= 2048 else 512


def _dense_ids(n_rows, tm, idx):
    return jnp.full((n_rows // tm,), idx, jnp.int32)


def _col_tile(k, tm, n_out, first_col, out_bytes):
    budget = VMEM_LIMIT - 12 * 1024 * 1024
    for tn in (1024, 768, 512, 256):
        if n_out % tn or first_col % tn:
            continue
        working = k * tn * (4 + 2) + 2 * tm * k * 2 + 2 * tm * tn * out_bytes + tm * tn * 4
        if working <= budget:
            return tn
    raise ValueError("no column tile fits")


def _dense(a3, w, idx, *, bias=None, out_dtype=BF16, first_col=0, n_out=None):
    parts = tuple(a3) if isinstance(a3, (tuple, list)) else (a3,)
    b, s, _ = parts[0].shape
    k = w.shape[1]
    m = b * s
    tm = min(_row_tile(k), m)
    n_out = w.shape[2] if n_out is None else n_out
    tn = _col_tile(k, tm, n_out, first_col, jnp.dtype(out_dtype).itemsize)
    out = gmm(tuple(p.reshape(m, p.shape[2]) for p in parts), w, _dense_ids(m, tm, idx), bias, tm=tm, tn=tn,
              out_dtype=out_dtype, col_off=first_col // tn, n_out=n_out)
    return out.reshape(b, s, -1)


def _dense_swiglu(h3, w1, w3, w2, idx):
    b, s, k = h3.shape
    m = b * s
    tm = min(1024, m)
    mid = swiglu_up(h3.reshape(m, k), w1, w3, _dense_ids(m, tm, idx), tm=tm, tn=512)
    tm2 = min(_row_tile(mid.shape[1]), m)
    return gmm(mid, w2, _dense_ids(m, tm2, idx), tm=tm2, tn=512, out_dtype=BF16).reshape(b, s, -1)


def _attention_layer(h, hc, i, w_in, w_out, q_g, k_g, bias, cos2, sin2, ctx_out):
    d = h.shape[-1]
    n_na = d // (2 * HEAD_DIM)
    n_q = n_na
    n_kv = max(1, n_q // 4)
    group = n_q // n_kv
    c_bq, c_ak, c_av = n_na, n_na + n_q, 2 * n_na + n_q
    c_bk, c_bv = 3 * n_na + n_q, 3 * n_na + n_q + n_kv
    q_cols = (n_na + n_q) * HEAD_DIM
    proj = _dense(h, w_in, i)
    if ctx_out:
        projc = _dense(hc, w_in, i)
        shift = 0
    else:
        projc = _dense(hc, w_in, i, first_col=q_cols, n_out=w_in.shape[2] - q_cols)
        shift = n_na + n_q
    qb = head_prep(proj, c_bq, n_q, q_g, cos2, sin2, rope=True)
    kb = head_prep(proj, c_bk, n_kv, k_g, cos2, sin2, rope=True)
    kbc = head_prep(projc, c_bk - shift, n_kv, k_g, cos2, sin2, rope=False)
    o_a = na_attention(proj, projc, n_na, 0, c_ak, c_av, c_ak - shift, c_av - shift, bias)
    o_b = full_attention(qb, 0, n_kv, group, [(kbc, 0), (kb, 0)], [(projc, c_bv - shift), (proj, c_bv)])
    y = _dense((o_a, o_b), w_out, i)
    if not ctx_out:
        return y, None
    qbc = head_prep(projc, c_bq, n_q, q_g, cos2, sin2, rope=False)
    oc_a = full_attention(projc, 0, n_na, 1, [(projc, c_ak)], [(projc, c_av)])
    oc_b = full_attention(qbc, 0, n_kv, group, [(kbc, 0)], [(projc, c_bv)])
    yc = _dense((oc_a, oc_b), w_out, i)
    return y, yc


def _hyena_layer(h, i, p):
    (w_in, b_in, conv_w, conv_b, f_w1, f_b1, f_w2, f_b2, f_w3, freq, skip, w_out, b_out) = p
    n_tok = h.shape[1]
    c = min(HY_CHUNK, n_tok)
    u0 = _dense(h, w_in, i, bias=b_in[i].reshape(1, -1))
    u = short_conv(u0, conv_w[i], conv_b[i])
    fwd_u, fwd_g, inv = dft_matrices(c)
    klag = hyena_lag_filters(n_tok, f_w1[i], f_b1[i], f_w2[i], f_b2[i], f_w3[i], freq[i])
    spectra = filter_spectra(klag, fwd_g, c)
    z = long_conv_gate(u, 0, u, 1, spectra, 0, skip[i], fwd_u, inv, c)
    z = long_conv_gate(z, 0, u, 2, spectra, 1, skip[i], fwd_u, inv, c)
    return _dense(z, w_out, i, bias=b_out[i].reshape(1, -1))


def kernel(x, c, ctx, c_ctx, mod_w, mod_b, norm_g, att_w_in, att_w_out, att_q_norm_g, att_k_norm_g, na_rpb,
           ffn_w1, ffn_w3, ffn_w2, hy_w_in, hy_b_in, hy_conv_w, hy_conv_b, hy_f_w1, hy_f_b1, hy_f_w2, hy_f_b2,
           hy_f_w3, hy_freq, hy_skip, hy_w_out, hy_b_out, moe_router, moe_w1, moe_w3, moe_w2):
    depth = mod_w.shape[0]
    b, s, d = x.shape
    n_ctx = ctx.shape[1]
    d_ff = moe_w1.shape[-1]
    cos2, sin2 = rope_tables(s)
    moe_w1f = moe_w1.reshape(-1, d, d_ff)
    moe_w3f = moe_w3.reshape(-1, d, d_ff)
    moe_w2f = moe_w2.reshape(-1, d_ff, d)
    hy = (hy_w_in, hy_b_in, hy_conv_w, hy_conv_b, hy_f_w1, hy_f_b1, hy_f_w2, hy_f_b2, hy_f_w3, hy_freq,
          hy_skip, hy_w_out, hy_b_out)

    n_mod_rows = 16
    cond = jnp.concatenate([c, c_ctx[None, :], jnp.zeros((n_mod_rows - b - 1, d), F32)], axis=0)

    lat_mods, ctx_mods = [], []
    for l in range(depth):
        mod = gmm(cond, mod_w, _dense_ids(n_mod_rows, n_mod_rows, l), mod_b[l].reshape(1, -1),
                  tm=n_mod_rows, tn=1024, out_dtype=F32, silu_in=True)
        m_lat = mod[:b].reshape(b, 6, 1, d)
        lat_mods.append([m_lat[:, k] for k in range(6)])
        m_ctx = jnp.broadcast_to(mod[b].reshape(1, 6, 1, d), (b, 6, 1, d))
        ctx_mods.append([m_ctx[:, k] for k in range(6)])

    xc = ctx
    h = modulate(x, norm_g[0, 0], lat_mods[0][0], lat_mods[0][1], out_dtype=BF16)
    for l in range(depth):
        even = l % 2 == 0
        i = l // 2
        ctx_out = any(m % 2 == 0 for m in range(l + 1, depth))
        ctx_in = even or ctx_out
        sh1, sc1, g1, sh2, sc2, g2 = lat_mods[l]
        csh1, csc1, cg1, csh2, csc2, cg2 = ctx_mods[l]

        hc = modulate(xc, norm_g[l, 0], csh1, csc1, out_dtype=BF16) if ctx_in else None
        if even:
            bias = na_bias_blocks(na_rpb[i])
            y, yc = _attention_layer(h, hc, i, att_w_in, att_w_out, att_q_norm_g[i], att_k_norm_g[i], bias,
                                     cos2, sin2, ctx_out)
        else:
            y = _hyena_layer(h, i, hy)
            yc = _hyena_layer(hc, i, hy) if ctx_out else None
        ffn_in = BF16 if even else F32
        x, h = residual_modulate(x, y, norm_g[l, 1], g1, norm_g[l, 2], sh2, sc2, out_dtype=ffn_in)
        if ctx_out:
            xc = residual(xc, yc, norm_g[l, 1], cg1)
            hc = modulate(xc, norm_g[l, 2], csh2, csc2, out_dtype=ffn_in)
        y_off = yc_off = 0
        if even:
            y = _dense_swiglu(h, ffn_w1, ffn_w3, ffn_w2, i)
            yc = _dense_swiglu(hc, ffn_w1, ffn_w3, ffn_w2, i) if ctx_out else None
        elif ctx_out and (b * n_ctx) % s == 0:
            tokens = jnp.concatenate([h.reshape(b * s, d), hc.reshape(b * n_ctx, d)], axis=0)
            y_all = moe_swiglu(tokens, moe_router[i], moe_w1f, moe_w3f, moe_w2f, i)
            y = y_all.reshape(-1, s, d)
            yc = y_all.reshape(-1, n_ctx, d)
            yc_off = b * s // n_ctx
        else:
            y = moe_swiglu(h.reshape(b * s, d), moe_router[i], moe_w1f, moe_w3f, moe_w2f, i).reshape(b, s, d)
            yc = None
            if ctx_out:
                yc = moe_swiglu(hc.reshape(b * n_ctx, d), moe_router[i], moe_w1f, moe_w3f, moe_w2f,
                                i).reshape(b, n_ctx, d)
        if l + 1 < depth:
            x, h = residual_modulate(x, y, norm_g[l, 3], g2, norm_g[l + 1, 0], lat_mods[l + 1][0],
                                     lat_mods[l + 1][1], out_dtype=BF16, y_off=y_off)
        else:
            x = residual(x, y, norm_g[l, 3], g2, y_off=y_off)
        if ctx_out:
            xc = residual(xc, yc, norm_g[l, 3], cg2, y_off=yc_off)
    return x
```

```python
import functools
import math

import jax
import jax.numpy as jnp
import numpy as np
from jax import lax
from jax.experimental import pallas as pl
from jax.experimental.pallas import tpu as pltpu

F32 = jnp.float32
BF16 = jnp.bfloat16
HIGHEST = lax.Precision.HIGHEST

GRID_W = 64
HEAD_DIM = 128
NA_WIN_ROWS = 8
NA_WIN_COLS = 16
ROPE_BASE = 10000.0
HY_ORDER = 2
HY_SHORT = 3
HY_EMB_DIM = 33
HY_TARGET = 1e-2
HY_FAST_DECAY = 0.3
HY_SLOW_DECAY = 1.5
HY_MOD_SHIFT = 0.05
N_EXPERTS = 8
NORM_EPS = 1e-6
NEG_INF = -1e30

LANES = 128
V7X_VMEM_BYTES = 64 * 1024 * 1024
VMEM_LIMIT = 56 * 1024 * 1024

ROW_TILE = 512
MOE_TILE = 512
GATHER_UNROLL = 8
HY_CHUNK = 512
NA_QROWS = 4
NA_KROWS = 12


def _cparams(sem):
    return pltpu.CompilerParams(dimension_semantics=sem, vmem_limit_bytes=VMEM_LIMIT)


def _weights_changed(be_ref, i):
    prev = be_ref[jnp.maximum(i - 1, 0)]
    return (i == 0) | (be_ref[i] != prev)


def _next_run_expert(block_e):
    nb = block_e.shape[0]
    idx = jnp.arange(nb, dtype=jnp.int32)
    run_last = jnp.concatenate([block_e[1:] != block_e[:-1], jnp.ones((1,), bool)])
    run_end = lax.cummin(jnp.where(run_last, idx, nb), axis=0, reverse=True)
    nxt = run_end + 1
    return jnp.where(nxt < nb, block_e[jnp.minimum(nxt, nb - 1)], -1).astype(jnp.int32)


def _stream_weights(be_ref, nxt_ref, bufs, sems, tn, col_off):
    j = pl.program_id(0)
    i = pl.program_id(1)
    n_col = pl.num_programs(0)

    def tile_copies(e, jj):
        cols = pl.ds(pl.multiple_of((jj + col_off) * tn, tn), tn)
        return [pltpu.make_async_copy(w.at[e, :, cols], stage, sems.at[n])
                for n, (w, stage, _) in enumerate(bufs)]

    @pl.when((j == 0) & (i == 0))
    def _():
        for cp in tile_copies(be_ref[0], 0):
            cp.start()

    @pl.when(_weights_changed(be_ref, i))
    def _():
        for cp, (_, stage, wbf) in zip(tile_copies(be_ref[i], j), bufs):
            cp.wait()
            wbf[...] = stage[...].astype(BF16)
        next_e = nxt_ref[i]

        @pl.when(next_e >= 0)
        def _():
            for cp in tile_copies(next_e, j):
                cp.start()

        @pl.when((next_e < 0) & (j + 1 < n_col))
        def _():
            for cp in tile_copies(be_ref[0], j + 1):
                cp.start()


def _gmm_kernel(be_ref, nu_ref, nxt_ref, *refs, n_a, silu_in, tn, col_off):
    a_refs = refs[:n_a]
    w_hbm, b_ref, o_ref, stage_ref, wbf_ref, sems = refs[n_a:]
    i = pl.program_id(1)
    used = i < nu_ref[0]
    _stream_weights(be_ref, nxt_ref, [(w_hbm, stage_ref, wbf_ref)], sems, tn, col_off)

    @pl.when(used)
    def _():
        acc = b_ref[...]
        k0 = 0
        for a_ref in a_refs:
            a = a_ref[...]
            if silu_in:
                a = a.astype(F32)
                a = a * jax.nn.sigmoid(a)
            k1 = k0 + a.shape[1]
            acc = acc + jnp.dot(a.astype(BF16), wbf_ref[k0:k1, :], preferred_element_type=F32)
            k0 = k1
        o_ref[...] = acc.astype(o_ref.dtype)

    @pl.when(jnp.logical_not(used))
    def _():
        o_ref[...] = jnp.zeros_like(o_ref)


def _all_used(block_e):
    return jnp.full((1,), block_e.shape[0], jnp.int32)


def gmm(a, w, block_e, bias=None, *, tm, tn, out_dtype, col_off=0, n_out=None, silu_in=False, n_used=None):
    a_parts = tuple(a) if isinstance(a, (tuple, list)) else (a,)
    m = a_parts[0].shape[0]
    k = sum(p.shape[1] for p in a_parts)
    assert k == w.shape[1]
    n_out = w.shape[2] if n_out is None else n_out
    if bias is None:
        bias = jnp.zeros((1, w.shape[2]), F32)
    n_used = _all_used(block_e) if n_used is None else n_used
    grid = (n_out // tn, m // tm)
    a_specs = [pl.BlockSpec((tm, p.shape[1]), lambda j, i, be, nu, nx: (jnp.minimum(i, nu[0] - 1), 0))
               for p in a_parts]
    return pl.pallas_call(
        functools.partial(_gmm_kernel, n_a=len(a_parts), silu_in=silu_in, tn=tn, col_off=col_off),
        out_shape=jax.ShapeDtypeStruct((m, n_out), out_dtype),
        grid_spec=pltpu.PrefetchScalarGridSpec(
            num_scalar_prefetch=3,
            grid=grid,
            in_specs=a_specs + [
                pl.BlockSpec(memory_space=pl.ANY),
                pl.BlockSpec((1, tn), lambda j, i, be, nu, nx: (0, j + col_off)),
            ],
            out_specs=pl.BlockSpec((tm, tn), lambda j, i, be, nu, nx: (i, j)),
            scratch_shapes=[pltpu.VMEM((k, tn), F32), pltpu.VMEM((k, tn), BF16),
                            pltpu.SemaphoreType.DMA((1,))],
        ),
        compiler_params=_cparams(("arbitrary", "arbitrary")),
        name="gmm",
    )(block_e, n_used, _next_run_expert(block_e), *a_parts, w, bias)


def _up_kernel(be_ref, nu_ref, nxt_ref, a_ref, w1_hbm, w3_hbm, o_ref, stage1_ref, stage3_ref, w1bf_ref,
               w3bf_ref, sems, *, tn):
    i = pl.program_id(1)
    used = i < nu_ref[0]
    _stream_weights(be_ref, nxt_ref, [(w1_hbm, stage1_ref, w1bf_ref), (w3_hbm, stage3_ref, w3bf_ref)],
                    sems, tn, 0)

    @pl.when(used)
    def _():
        a = a_ref[...]
        g = jnp.dot(a, w1bf_ref[...], preferred_element_type=F32)
        u = jnp.dot(a, w3bf_ref[...], preferred_element_type=F32)
        o_ref[...] = (g * jax.nn.sigmoid(g) * u).astype(o_ref.dtype)

    @pl.when(jnp.logical_not(used))
    def _():
        o_ref[...] = jnp.zeros_like(o_ref)


def swiglu_up(a, w1, w3, block_e, *, tm, tn, n_used=None):
    m, k = a.shape
    n = w1.shape[2]
    n_used = _all_used(block_e) if n_used is None else n_used
    grid = (n // tn, m // tm)
    return pl.pallas_call(
        functools.partial(_up_kernel, tn=tn),
        out_shape=jax.ShapeDtypeStruct((m, n), BF16),
        grid_spec=pltpu.PrefetchScalarGridSpec(
            num_scalar_prefetch=3,
            grid=grid,
            in_specs=[
                pl.BlockSpec((tm, k), lambda j, i, be, nu, nx: (jnp.minimum(i, nu[0] - 1), 0)),
                pl.BlockSpec(memory_space=pl.ANY),
                pl.BlockSpec(memory_space=pl.ANY),
            ],
            out_specs=pl.BlockSpec((tm, tn), lambda j, i, be, nu, nx: (i, j)),
            scratch_shapes=[pltpu.VMEM((k, tn), F32), pltpu.VMEM((k, tn), F32),
                            pltpu.VMEM((k, tn), BF16), pltpu.VMEM((k, tn), BF16),
                            pltpu.SemaphoreType.DMA((2,))],
        ),
        compiler_params=_cparams(("arbitrary", "arbitrary")),
        name="swiglu_up",
    )(block_e, n_used, _next_run_expert(block_e), a, w1, w3)


def _rms(x, g):
    ms = jnp.mean(x * x, axis=-1, keepdims=True)
    return x * lax.rsqrt(ms + NORM_EPS) * g


def _mod_kernel(x_ref, g_ref, sh_ref, sc_ref, o_ref):
    y = _rms(x_ref[0], g_ref[...])
    o_ref[0] = (y * (1.0 + sc_ref[0]) + sh_ref[0]).astype(o_ref.dtype)


def modulate(x, g, shift, scale, *, out_dtype, tr=ROW_TILE):
    b, s, d = x.shape
    tr = min(tr, s)
    row = pl.BlockSpec((1, tr, d), lambda bi, si: (bi, si, 0))
    vec = pl.BlockSpec((1, 1, d), lambda bi, si: (bi, 0, 0))
    return pl.pallas_call(
        _mod_kernel,
        out_shape=jax.ShapeDtypeStruct((b, s, d), out_dtype),
        grid=(b, s // tr),
        in_specs=[row, pl.BlockSpec((1, d), lambda bi, si: (0, 0)), vec, vec],
        out_specs=row,
        compiler_params=_cparams(("parallel", "parallel")),
        name="modulate",
    )(x, g.reshape(1, d), shift, scale)


def _resid_kernel(x_ref, y_ref, g_ref, gate_ref, o_ref):
    y = _rms(y_ref[0].astype(F32), g_ref[...])
    o_ref[0] = x_ref[0] + gate_ref[0] * y


def residual(x, y, g, gate, *, tr=ROW_TILE, y_off=0):
    b, s, d = x.shape
    tr = min(tr, s)
    row = pl.BlockSpec((1, tr, d), lambda bi, si: (bi, si, 0))
    y_row = pl.BlockSpec((1, tr, d), lambda bi, si: (bi + y_off, si, 0))
    vec = pl.BlockSpec((1, 1, d), lambda bi, si: (bi, 0, 0))
    return pl.pallas_call(
        _resid_kernel,
        out_shape=jax.ShapeDtypeStruct((b, s, d), F32),
        grid=(b, s // tr),
        in_specs=[row, y_row, pl.BlockSpec((1, d), lambda bi, si: (0, 0)), vec],
        out_specs=row,
        compiler_params=_cparams(("parallel", "parallel")),
        name="residual",
    )(x, y, g.reshape(1, d), gate)


def _resid_mod_kernel(x_ref, y_ref, gy_ref, gate_ref, gh_ref, sh_ref, sc_ref, xo_ref, ho_ref):
    x = x_ref[0] + gate_ref[0] * _rms(y_ref[0].astype(F32), gy_ref[...])
    xo_ref[0] = x
    ho_ref[0] = (_rms(x, gh_ref[...]) * (1.0 + sc_ref[0]) + sh_ref[0]).astype(ho_ref.dtype)


def residual_modulate(x, y, g_y, gate, g_h, shift, scale, *, out_dtype, tr=ROW_TILE, y_off=0):
    b, s, d = x.shape
    tr = min(tr, s)
    row = pl.BlockSpec((1, tr, d), lambda bi, si: (bi, si, 0))
    y_row = pl.BlockSpec((1, tr, d), lambda bi, si: (bi + y_off, si, 0))
    vec = pl.BlockSpec((1, 1, d), lambda bi, si: (bi, 0, 0))
    gain = pl.BlockSpec((1, d), lambda bi, si: (0, 0))
    return pl.pallas_call(
        _resid_mod_kernel,
        out_shape=(jax.ShapeDtypeStruct((b, s, d), F32), jax.ShapeDtypeStruct((b, s, d), out_dtype)),
        grid=(b, s // tr),
        in_specs=[row, y_row, gain, vec, gain, vec, vec],
        out_specs=(row, row),
        compiler_params=_cparams(("parallel", "parallel")),
        name="residual_modulate",
    )(x, y, g_y.reshape(1, d), gate, g_h.reshape(1, d), shift, scale)


def _prep_kernel(x_ref, g_ref, cos_ref, sin_ref, o_ref, *, rope, n_heads):
    for h in range(n_heads):
        cols = slice(h * HEAD_DIM, (h + 1) * HEAD_DIM)
        y = _rms(x_ref[0, :, cols].astype(F32), g_ref[...])
        if rope:
            rot = pltpu.roll(y, HEAD_DIM // 2, axis=1)
            y = y * cos_ref[...] + rot * sin_ref[...]
        o_ref[0, :, cols] = y.astype(o_ref.dtype)


def head_prep(src, col0, n_heads, g, cos2, sin2, *, rope):
    b, s, _ = src.shape
    ts = min(s, 512)
    width = n_heads * HEAD_DIM
    assert col0 % n_heads == 0
    return pl.pallas_call(
        functools.partial(_prep_kernel, rope=rope, n_heads=n_heads),
        out_shape=jax.ShapeDtypeStruct((b, s, width), BF16),
        grid=(b, s // ts),
        in_specs=[
            pl.BlockSpec((1, ts, width), lambda bi, si: (bi, si, col0 // n_heads)),
            pl.BlockSpec((1, HEAD_DIM), lambda bi, si: (0, 0)),
            pl.BlockSpec((ts, HEAD_DIM), lambda bi, si: (si, 0)),
            pl.BlockSpec((ts, HEAD_DIM), lambda bi, si: (si, 0)),
        ],
        out_specs=pl.BlockSpec((1, ts, width), lambda bi, si: (bi, si, 0)),
        compiler_params=_cparams(("parallel", "parallel")),
        name="head_prep",
    )(src, g.reshape(1, HEAD_DIM), cos2[:s], sin2[:s])


def _nt_dot(q, k):
    return lax.dot_general(q, k, (((1,), (1,)), ((), ())), preferred_element_type=F32)


def _with_ones(v):
    return jnp.concatenate([v, jnp.ones(v.shape, v.dtype)], axis=1)


def _softmax_pv(scores, values_ext, out_dtype):
    m = scores[0].max(axis=-1, keepdims=True)
    for s in scores[1:]:
        m = jnp.maximum(m, s.max(axis=-1, keepdims=True))
    ps = [jnp.exp((s - m).astype(BF16)) for s in scores]
    acc = jnp.dot(ps[0], values_ext[0], preferred_element_type=F32)
    for p, v in zip(ps[1:], values_ext[1:]):
        acc = acc + jnp.dot(p, v, preferred_element_type=F32)
    return (acc[:, :HEAD_DIM] / acc[:, HEAD_DIM:HEAD_DIM + 1]).astype(out_dtype)


def _scaled(q):
    return (q.astype(F32) * (HEAD_DIM ** -0.5)).astype(BF16)


def _attn_kernel(*refs, group, n_parts):
    q_ref = refs[0]
    k_refs = refs[1:1 + n_parts]
    v_refs = refs[1 + n_parts:1 + 2 * n_parts]
    o_ref = refs[1 + 2 * n_parts]
    ks = [r[0] for r in k_refs]
    vs = [_with_ones(r[0]) for r in v_refs]
    for g in range(group):
        cols = slice(g * HEAD_DIM, (g + 1) * HEAD_DIM)
        q = _scaled(q_ref[0, :, cols])
        scores = [_nt_dot(q, k) for k in ks]
        o_ref[0, :, cols] = _softmax_pv(scores, vs, o_ref.dtype)


def full_attention(q_src, q_col0, n_kv, group, k_parts, v_parts, *, tq=512):
    b, sq, _ = q_src.shape
    tq = min(tq, sq)
    gw = group * HEAD_DIM
    assert (q_col0 * HEAD_DIM) % gw == 0
    q_blk0 = q_col0 * HEAD_DIM // gw
    in_specs = [pl.BlockSpec((1, tq, gw), lambda bi, h, qi: (bi, qi, q_blk0 + h))]
    args = [q_src]
    for arr, c0 in list(k_parts) + list(v_parts):
        in_specs.append(pl.BlockSpec((1, arr.shape[1], HEAD_DIM),
                                     functools.partial(lambda bi, h, qi, c0: (bi, 0, c0 + h), c0=c0)))
        args.append(arr)
    return pl.pallas_call(
        functools.partial(_attn_kernel, group=group, n_parts=len(k_parts)),
        out_shape=jax.ShapeDtypeStruct((b, sq, n_kv * gw), BF16),
        grid=(b, n_kv, sq // tq),
        in_specs=in_specs,
        out_specs=pl.BlockSpec((1, tq, gw), lambda bi, h, qi: (bi, qi, h)),
        compiler_params=_cparams(("parallel", "parallel", "parallel")),
        name="full_attention",
    )(*args)


NA_MASKED = 2 * NA_WIN_ROWS - 1


def _na_bias(bias_ref, r0, start, rows):
    lane = lax.broadcasted_iota(jnp.int32, (GRID_W, 2 * GRID_W), 1)
    bands = []
    for i in range(NA_QROWS):
        r = r0 + i
        rs = min(max(r - NA_WIN_ROWS // 2, 0), rows - NA_WIN_ROWS)

        def block_id(j):
            kr = start + j
            return kr - r + NA_WIN_ROWS - 1 if rs <= kr < rs + NA_WIN_ROWS else NA_MASKED

        tiles = []
        for m in range(NA_KROWS // 2):
            d0, d1 = block_id(2 * m), block_id(2 * m + 1)
            tile = bias_ref[0, d0]
            if d1 != d0:
                tile = jnp.where(lane < GRID_W, tile, bias_ref[0, d1])
            tiles.append(tile)
        bands.append(jnp.concatenate(tiles, axis=1))
    return jnp.concatenate(bands, axis=0)


def _na_kernel(q_ref, k_ref, v_ref, kc_ref, vc_ref, bias_ref, o_ref, *, rows):
    n_steps = rows // NA_QROWS
    kc = kc_ref[0]
    vc = _with_ones(vc_ref[0])
    for step in range(n_steps):
        r0 = step * NA_QROWS
        start = min(max(r0 - NA_WIN_ROWS // 2, 0), rows - NA_KROWS)
        qs = slice(r0 * GRID_W, (r0 + NA_QROWS) * GRID_W)
        kslice = slice(start * GRID_W, (start + NA_KROWS) * GRID_W)
        q = _scaled(q_ref[0, qs, :])
        s_lat = _nt_dot(q, k_ref[0, kslice, :]) + _na_bias(bias_ref, r0, start, rows)
        s_ctx = _nt_dot(q, kc)
        o_ref[0, qs, :] = _softmax_pv([s_lat, s_ctx], [_with_ones(v_ref[0, kslice, :]), vc], o_ref.dtype)


def na_bias_blocks(rpb):
    n_dc = 2 * NA_WIN_COLS - 1
    col = np.arange(GRID_W)
    col_start = np.clip(col - NA_WIN_COLS // 2, 0, GRID_W - NA_WIN_COLS)
    col_ok = (col[None, :] >= col_start[:, None]) & (col[None, :] < col_start[:, None] + NA_WIN_COLS)
    dc = np.clip(col[None, :] - col[:, None] + NA_WIN_COLS - 1, 0, n_dc - 1)
    pick_dc = (dc[:, :, None] == np.arange(n_dc)).astype(np.float32)
    blocks = jnp.einsum('hde,qke->hdqk', rpb.astype(F32), pick_dc, precision=HIGHEST)
    blocks = jnp.where(col_ok[None, None], blocks, NEG_INF)
    masked = jnp.full((rpb.shape[0], 1, GRID_W, GRID_W), NEG_INF, F32)
    blocks = jnp.concatenate([blocks, masked], axis=1)
    return jnp.concatenate([blocks, blocks], axis=-1)


def na_attention(proj, projc, n_heads, q_col0, k_col0, v_col0, kc_col0, vc_col0, bias):
    b, s, _ = proj.shape
    n_ctx = projc.shape[1]
    rows = s // GRID_W
    assert rows % NA_QROWS == 0 and NA_KROWS % 2 == 0 and 2 * GRID_W == LANES
    lat = lambda c0: pl.BlockSpec((1, s, HEAD_DIM), lambda h, bi: (bi, 0, c0 + h))
    ctx = lambda c0: pl.BlockSpec((1, n_ctx, HEAD_DIM), lambda h, bi: (bi, 0, c0 + h))
    return pl.pallas_call(
        functools.partial(_na_kernel, rows=rows),
        out_shape=jax.ShapeDtypeStruct((b, s, n_heads * HEAD_DIM), BF16),
        grid=(n_heads, b),
        in_specs=[lat(q_col0), lat(k_col0), lat(v_col0), ctx(kc_col0), ctx(vc_col0),
                  pl.BlockSpec((1,) + bias.shape[1:], lambda h, bi: (h, 0, 0, 0))],
        out_specs=pl.BlockSpec((1, s, HEAD_DIM), lambda h, bi: (bi, 0, h)),
        compiler_params=_cparams(("parallel", "parallel")),
        name="na_attention",
    )(proj, proj, proj, projc, projc, bias)


def rope_tables(n_tok):
    t = jnp.arange(n_tok)
    row = (t // GRID_W).astype(F32)
    col = (t % GRID_W).astype(F32)
    n_freq = HEAD_DIM // 4
    inv_freq = ROPE_BASE ** (-jnp.arange(n_freq, dtype=F32) / n_freq)
    ang = jnp.concatenate([row[:, None] * inv_freq, col[:, None] * inv_freq], axis=-1)
    cos, sin = jnp.cos(ang), jnp.sin(ang)
    return jnp.concatenate([cos, cos], axis=-1), jnp.concatenate([-sin, sin], axis=-1)


def _shortconv_kernel(u_ref, w_ref, b_ref, o_ref):
    u = u_ref[0].astype(F32)
    n = u.shape[0]
    row = lax.broadcasted_iota(jnp.int32, u.shape, 0)
    prev = jnp.where(row == 0, 0.0, pltpu.roll(u, 1, axis=0))
    nxt = jnp.where(row == n - 1, 0.0, pltpu.roll(u, n - 1, axis=0))
    o = prev * w_ref[0:1, :] + u * w_ref[1:2, :] + nxt * w_ref[2:3, :] + b_ref[...]
    o_ref[0] = o.astype(o_ref.dtype)


def short_conv(u, w, bias, *, tc=256):
    b, n, c = u.shape
    return pl.pallas_call(
        _shortconv_kernel,
        out_shape=jax.ShapeDtypeStruct((b, n, c), BF16),
        grid=(b, c // tc),
        in_specs=[pl.BlockSpec((1, n, tc), lambda bi, j: (bi, 0, j)),
                  pl.BlockSpec((HY_SHORT, tc), lambda bi, j: (0, j)),
                  pl.BlockSpec((1, tc), lambda bi, j: (0, j))],
        out_specs=pl.BlockSpec((1, n, tc), lambda bi, j: (bi, 0, j)),
        compiler_params=_cparams(("parallel", "parallel")),
        name="short_conv",
    )(u, w, bias.reshape(1, c))


def _filter_mlp_kernel(z_ref, w1_ref, b1_ref, w2_ref, b2_ref, fr_ref, o_ref):
    fr = fr_ref[...]
    a = jnp.sin(fr * (jnp.dot(z_ref[...], w1_ref[...], precision=HIGHEST, preferred_element_type=F32) + b1_ref[...]))
    o_ref[...] = jnp.sin(fr * (jnp.dot(a, w2_ref[...], precision=HIGHEST, preferred_element_type=F32) + b2_ref[...]))


def _split_bf16(x):
    hi = x.astype(BF16)
    return hi, (x - hi.astype(F32)).astype(BF16)


def _dot3(a, w):
    a_hi, a_lo = _split_bf16(a)
    w_hi, w_lo = _split_bf16(w)
    return (jnp.dot(a_hi, w_hi, preferred_element_type=F32) + jnp.dot(a_lo, w_hi, preferred_element_type=F32)
            + jnp.dot(a_hi, w_lo, preferred_element_type=F32))


def _filter_kernel(a_ref, t_ref, w3f_ref, w3b_ref, dl_ref, o_ref, *, n_tok):
    a = a_ref[...]
    f_fwd = _dot3(a, w3f_ref[...])
    f_bwd = _dot3(a, w3b_ref[...])
    window = jnp.exp(-t_ref[...] * dl_ref[...]) + HY_MOD_SHIFT
    lag_row = lax.broadcasted_iota(jnp.int32, f_fwd.shape, 0)
    filt = jnp.where(lag_row >= n_tok, f_fwd, f_bwd) * window
    o_ref[0] = jnp.where(lag_row == 0, 0.0, filt)


def hyena_lag_filters(n_tok, f_w1, f_b1, f_w2, f_b2, f_w3, freq, *, tc=256):
    d = f_w3.shape[1] // (2 * HY_ORDER)
    hidden = f_w1.shape[1]
    bands = (HY_EMB_DIM - 1) // 2
    pos = jnp.abs(jnp.arange(2 * n_tok) - n_tok)
    pos = jnp.minimum(pos, n_tok - 1)
    t = jnp.linspace(0.0, 1.0, n_tok, dtype=F32)[pos][:, None]
    w = (2.0 * math.pi * jnp.arange(n_tok, dtype=F32) / n_tok)[pos][:, None]
    f = jnp.linspace(1e-4, bands - 1, bands, dtype=F32)[None, :]
    z = jnp.concatenate([t, jnp.cos(f * w), -jnp.sin(f * w)], axis=-1)
    z = jnp.pad(z, ((0, 0), (0, LANES - HY_EMB_DIM)))
    w1 = jnp.pad(f_w1.astype(F32), ((0, LANES - HY_EMB_DIM), (0, 0)))
    max_decay = math.log(HY_TARGET) / HY_FAST_DECAY
    min_decay = math.log(HY_TARGET) / HY_SLOW_DECAY
    deltas = jnp.abs(jnp.linspace(min_decay, max_decay, d, dtype=F32)).reshape(1, d)
    nj = d // tc
    n_lag = 2 * n_tok
    one = lambda shape: pl.BlockSpec(shape, lambda i: (0,) * len(shape))
    feats = pl.pallas_call(
        _filter_mlp_kernel,
        out_shape=jax.ShapeDtypeStruct((n_lag, hidden), F32),
        grid=(1,),
        in_specs=[one((n_lag, LANES)), one((LANES, hidden)), one((1, hidden)), one((hidden, hidden)),
                  one((1, hidden)), one((1, hidden))],
        out_specs=one((n_lag, hidden)),
        compiler_params=_cparams(("arbitrary",)),
        name="hyena_filter_mlp",
    )(z, w1, f_b1.reshape(1, hidden).astype(F32), f_w2.astype(F32), f_b2.reshape(1, hidden).astype(F32),
      freq.reshape(1, hidden).astype(F32))
    full = lambda shape: pl.BlockSpec(shape, lambda o, j: (0,) * len(shape))
    return pl.pallas_call(
        functools.partial(_filter_kernel, n_tok=n_tok),
        out_shape=jax.ShapeDtypeStruct((HY_ORDER, n_lag, d), F32),
        grid=(HY_ORDER, nj),
        in_specs=[full((n_lag, hidden)), full((n_lag, 1)),
                  pl.BlockSpec((hidden, tc), lambda o, j: (0, (2 * o) * nj + j)),
                  pl.BlockSpec((hidden, tc), lambda o, j: (0, (2 * o + 1) * nj + j)),
                  pl.BlockSpec((1, tc), lambda o, j: (0, j))],
        out_specs=pl.BlockSpec((1, n_lag, tc), lambda o, j: (o, 0, j)),
        compiler_params=_cparams(("parallel", "parallel")),
        name="hyena_filters",
    )(feats, t, f_w3.astype(F32), f_w3.astype(F32), deltas)


def _dft_angles(f, n, c):
    k = jnp.mod((2 * f + 1) * n, 4 * c).astype(F32)
    return k * (2.0 * math.pi / (4 * c))


def dft_matrices(c):
    f = jnp.arange(c)[:, None]
    ang_u = _dft_angles(f, jnp.arange(c)[None, :], c)
    fwd_u = jnp.concatenate([jnp.cos(ang_u), -jnp.sin(ang_u)], axis=0)
    ang_g = _dft_angles(f, jnp.arange(-c, c)[None, :], c)
    fwd_g = jnp.concatenate([jnp.cos(ang_g), -jnp.sin(ang_g)], axis=0)
    inv = jnp.concatenate([jnp.cos(ang_u).T, -jnp.sin(ang_u).T], axis=1) / c
    return fwd_u.astype(BF16), fwd_g.astype(BF16), inv.astype(BF16)


def _spectra_kernel(fa_ref, fb_ref, ka_ref, kb_ref, o_ref):
    acc = jnp.dot(fa_ref[...], ka_ref[0].astype(BF16), preferred_element_type=F32)
    acc = acc + jnp.dot(fb_ref[...], kb_ref[0].astype(BF16), preferred_element_type=F32)
    o_ref[0, 0] = acc.astype(o_ref.dtype)


def filter_spectra(klag, fwd_g, c, *, tc=1024):
    n_ord, two_l, d = klag.shape
    n_delta = two_l // c - 1
    fa, fb = fwd_g[:, :c], fwd_g[:, c:]
    return pl.pallas_call(
        _spectra_kernel,
        out_shape=jax.ShapeDtypeStruct((n_ord, n_delta, 2 * c, d), BF16),
        grid=(n_ord, n_delta, d // tc),
        in_specs=[pl.BlockSpec((2 * c, c), lambda o, dl, j: (0, 0)),
                  pl.BlockSpec((2 * c, c), lambda o, dl, j: (0, 0)),
                  pl.BlockSpec((1, c, tc), lambda o, dl, j: (o, dl, j)),
                  pl.BlockSpec((1, c, tc), lambda o, dl, j: (o, dl + 1, j))],
        out_specs=pl.BlockSpec((1, 1, 2 * c, tc), lambda o, dl, j: (o, dl, 0, j)),
        compiler_params=_cparams(("parallel", "parallel", "parallel")),
        name="filter_spectra",
    )(fa, fb, klag, klag)


def _longconv_kernel(u_ref, gate_ref, sk_ref, skip_ref, fu_ref, inv_ref, o_ref, spec_ref, y_ref, *, c, n_ch, order):
    for j in range(n_ch):
        spec_ref[j] = jnp.dot(fu_ref[...], u_ref[0, j * c:(j + 1) * c, :],
                              preferred_element_type=F32).astype(BF16)
    skip = skip_ref[order:order + 1, :]
    for i in range(n_ch):
        top = None
        bot = None
        for j in range(n_ch):
            dl = i - j + n_ch - 1
            xr = spec_ref[j, :c, :]
            xi = spec_ref[j, c:, :]
            gr = sk_ref[0, dl, :c, :]
            gi = sk_ref[0, dl, c:, :]
            t = xr * gr - xi * gi
            b = xr * gi + xi * gr
            top = t if top is None else top + t
            bot = b if bot is None else bot + b
        y_ref[i, :c, :] = top
        y_ref[i, c:, :] = bot
        y = jnp.dot(inv_ref[...], y_ref[i], preferred_element_type=F32)
        rows = slice(i * c, (i + 1) * c)
        u = u_ref[0, rows, :].astype(F32)
        o_ref[0, rows, :] = (gate_ref[0, rows, :].astype(F32) * (y + u * skip)).astype(o_ref.dtype)


def long_conv_gate(u_src, u_col0, gate_src, gate_col0, spectra, order, skip, fwd_u, inv, c, *, tc=512):
    b, n, _ = u_src.shape
    d = spectra.shape[3]
    n_ch = n // c
    n_delta = spectra.shape[1]
    nj = d // tc
    return pl.pallas_call(
        functools.partial(_longconv_kernel, c=c, n_ch=n_ch, order=order),
        out_shape=jax.ShapeDtypeStruct((b, n, d), BF16),
        grid=(nj, b),
        in_specs=[pl.BlockSpec((1, n, tc), lambda j, bi: (bi, 0, u_col0 * nj + j)),
                  pl.BlockSpec((1, n, tc), lambda j, bi: (bi, 0, gate_col0 * nj + j)),
                  pl.BlockSpec((1, n_delta, 2 * c, tc), lambda j, bi: (order, 0, 0, j)),
                  pl.BlockSpec((HY_ORDER, tc), lambda j, bi: (0, j)),
                  pl.BlockSpec((2 * c, c), lambda j, bi: (0, 0)),
                  pl.BlockSpec((c, 2 * c), lambda j, bi: (0, 0))],
        out_specs=pl.BlockSpec((1, n, tc), lambda j, bi: (bi, 0, j)),
        scratch_shapes=[pltpu.VMEM((n_ch, 2 * c, tc), BF16), pltpu.VMEM((n_ch, 2 * c, tc), BF16)],
        compiler_params=_cparams(("parallel", "parallel")),
        name="long_conv_gate",
    )(u_src, gate_src, spectra, skip.astype(F32), fwd_u, inv)


def _router_kernel(h_ref, w_ref, o_ref):
    logits = _dot3(h_ref[...], w_ref[...])
    lane = lax.broadcasted_iota(jnp.int32, logits.shape, 1).astype(F32)
    lg = jnp.where(lane < N_EXPERTS, logits, -jnp.inf)
    m1 = lg.max(axis=-1, keepdims=True)
    e1 = jnp.where(lg == m1, lane, float(LANES)).min(axis=-1, keepdims=True)
    lg2 = jnp.where(lane == e1, -jnp.inf, lg)
    m2 = lg2.max(axis=-1, keepdims=True)
    e2 = jnp.where(lg2 == m2, lane, float(LANES)).min(axis=-1, keepdims=True)
    t = jnp.exp(m2 - m1)
    p1 = 1.0 / (1.0 + t)
    p2 = t / (1.0 + t)
    o_ref[...] = jnp.where(lane == 0, e1, jnp.where(lane == 1, e2, jnp.where(lane == 2, p1,
                           jnp.where(lane == 3, p2, 0.0))))


def moe_route(h, router_w, *, tm=512):
    t, d = h.shape
    w = jnp.pad(router_w.astype(F32), ((0, 0), (0, LANES - N_EXPERTS)))
    return pl.pallas_call(
        _router_kernel,
        out_shape=jax.ShapeDtypeStruct((t, LANES), F32),
        grid=(t // tm,),
        in_specs=[pl.BlockSpec((tm, d), lambda i: (i, 0)), pl.BlockSpec((d, LANES), lambda i: (0, 0))],
        out_specs=pl.BlockSpec((tm, LANES), lambda i: (i, 0)),
        compiler_params=_cparams(("parallel",)),
        name="moe_router",
    )(h, w)


def _row_copy(src_ref, dst_ref, sem, src_row, dst_row):
    return pltpu.make_async_copy(src_ref.at[pl.ds(src_row, 1)], dst_ref.at[pl.ds(dst_row, 1)], sem)


def _load_indices(idx_vmem_ref, idx_smem, isem):
    cp = pltpu.make_async_copy(idx_vmem_ref, idx_smem, isem)
    cp.start()
    cp.wait()


def _issue_rows(idx_smem, idx_row, src_ref, buf_ref, sem, n):
    def issue(blk, carry):
        for u in range(GATHER_UNROLL):
            r = blk * GATHER_UNROLL + u
            _row_copy(src_ref, buf_ref, sem, idx_smem[idx_row, r], r).start(priority=u % 2)
        return carry

    lax.fori_loop(0, n // GATHER_UNROLL, issue, 0)


def _wait_rows(src_ref, buf_ref, sem, n):
    pltpu.make_async_copy(src_ref.at[pl.ds(0, n)], buf_ref, sem).wait()


def _gather_one_step_ahead(step, n_active, idx_refs, idx_next_refs, src_ref, idx_smem, buf_ref, sems, isem, tm):
    n_idx = len(idx_refs)
    slot = lax.rem(step, 2)

    def start(refs, s):
        for q, idx_ref in enumerate(refs):
            row = s * n_idx + q
            _load_indices(idx_ref.at[0], idx_smem.at[pl.ds(row, 1)], isem)
            _issue_rows(idx_smem, row, src_ref, buf_ref.at[s, q], sems.at[s, q], tm)

    @pl.when(step == 0)
    def _():
        start(idx_refs, 0)

    @pl.when(step + 1 < n_active)
    def _():
        start(idx_next_refs, 1 - slot)

    for q in range(n_idx):
        _wait_rows(src_ref, buf_ref.at[slot, q], sems.at[slot, q], tm)
    return slot


def _dispatch_kernel(nu_ref, idx_ref, idx_next_ref, src_ref, o_ref, idx_smem, buf_ref, sems, isem, *, tm):
    step = pl.program_id(0)
    used = step < nu_ref[0]

    @pl.when(used)
    def _():
        slot = _gather_one_step_ahead(step, nu_ref[0], [idx_ref], [idx_next_ref], src_ref, idx_smem, buf_ref,
                                      sems, isem, tm)
        o_ref[...] = buf_ref[slot, 0].astype(o_ref.dtype)

    @pl.when(jnp.logical_not(used))
    def _():
        o_ref[...] = jnp.zeros_like(o_ref)


def moe_dispatch(h, buf_tok, n_used, *, tm=MOE_TILE):
    _, d = h.shape
    n_rows = buf_tok.shape[0]
    n_blocks = n_rows // tm
    idx = buf_tok.reshape(n_blocks, 1, tm)
    return pl.pallas_call(
        functools.partial(_dispatch_kernel, tm=tm),
        out_shape=jax.ShapeDtypeStruct((n_rows, d), BF16),
        grid_spec=pltpu.PrefetchScalarGridSpec(
            num_scalar_prefetch=1,
            grid=(n_blocks,),
            in_specs=[pl.BlockSpec((1, 1, tm), lambda i, nu: (i, 0, 0)),
                      pl.BlockSpec((1, 1, tm), lambda i, nu: (jnp.minimum(i + 1, n_blocks - 1), 0, 0)),
                      pl.BlockSpec(memory_space=pl.ANY)],
            out_specs=pl.BlockSpec((tm, d), lambda i, nu: (i, 0)),
            scratch_shapes=[pltpu.SMEM((2, tm), jnp.int32), pltpu.VMEM((2, 1, tm, d), F32),
                            pltpu.SemaphoreType.DMA((2, 1)), pltpu.SemaphoreType.DMA],
        ),
        compiler_params=_cparams(("arbitrary",)),
        name="moe_dispatch",
    )(n_used, idx, idx, h)


def _combine_kernel(i0_ref, i1_ref, i0_next_ref, i1_next_ref, yb_ref, slab_ref, o_ref, idx_smem, buf_ref,
                    sems, isem, *, tm):
    slot = _gather_one_step_ahead(pl.program_id(0), pl.num_programs(0), [i0_ref, i1_ref],
                                  [i0_next_ref, i1_next_ref], yb_ref, idx_smem, buf_ref, sems, isem, tm)
    o_ref[...] = buf_ref[slot, 0] * slab_ref[:, 2:3] + buf_ref[slot, 1] * slab_ref[:, 3:4]


def moe_combine(yb, pos0, pos1, slab, *, tm=512):
    t = pos0.shape[0]
    d = yb.shape[1]
    n_blocks = t // tm
    idx_spec = pl.BlockSpec((1, 1, tm), lambda i: (i, 0, 0))
    idx_next_spec = pl.BlockSpec((1, 1, tm), lambda i: (jnp.minimum(i + 1, n_blocks - 1), 0, 0))
    idx0 = pos0.reshape(n_blocks, 1, tm)
    idx1 = pos1.reshape(n_blocks, 1, tm)
    return pl.pallas_call(
        functools.partial(_combine_kernel, tm=tm),
        out_shape=jax.ShapeDtypeStruct((t, d), F32),
        grid=(n_blocks,),
        in_specs=[idx_spec, idx_spec, idx_next_spec, idx_next_spec, pl.BlockSpec(memory_space=pl.ANY),
                  pl.BlockSpec((tm, LANES), lambda i: (i, 0))],
        out_specs=pl.BlockSpec((tm, d), lambda i: (i, 0)),
        scratch_shapes=[pltpu.SMEM((4, tm), jnp.int32), pltpu.VMEM((2, 2, tm, d), F32),
                        pltpu.SemaphoreType.DMA((2, 2)), pltpu.SemaphoreType.DMA],
        compiler_params=_cparams(("arbitrary",)),
        name="moe_combine",
    )(idx0, idx1, idx0, idx1, yb, slab)


def moe_routing_tables(slab, tile):
    t = slab.shape[0]
    flat_e = slab[:, :2].astype(jnp.int32).reshape(-1)
    n_assign = flat_e.shape[0]
    onehot = (flat_e[:, None] == jnp.arange(N_EXPERTS)[None, :]).astype(jnp.int32)
    csum = jnp.cumsum(onehot, axis=0)
    counts = csum[-1]
    rank = jnp.take_along_axis(csum, flat_e[:, None], axis=1)[:, 0] - 1
    padded = (counts + tile - 1) // tile * tile
    pad_end = jnp.cumsum(padded)
    pad_start = pad_end - padded
    dest = (pad_start[flat_e] + rank).astype(jnp.int32)
    n_blocks = (n_assign + N_EXPERTS * (tile - 1) + tile - 1) // tile
    buf_tok = jnp.zeros((n_blocks * tile,), jnp.int32).at[dest].set(
        jnp.arange(n_assign, dtype=jnp.int32) // 2, unique_indices=True)
    block_e = jnp.minimum(jnp.searchsorted(pad_end, jnp.arange(n_blocks) * tile, side='right'),
                          N_EXPERTS - 1).astype(jnp.int32)
    dest = dest.reshape(t, 2)
    n_used = (pad_end[-1:] // tile).astype(jnp.int32)
    return buf_tok, block_e, n_used, dest[:, 0], dest[:, 1]


def moe_swiglu(h, router_w, w1, w3, w2, layer):
    slab = moe_route(h, router_w)
    buf_tok, block_e, n_used, pos0, pos1 = moe_routing_tables(slab, MOE_TILE)
    block_e = block_e + layer * N_EXPERTS
    xb = moe_dispatch(h, buf_tok, n_used)
    mid = swiglu_up(xb, w1, w3, block_e, tm=MOE_TILE, tn=512, n_used=n_used)
    yb = gmm(mid, w2, block_e, tm=MOE_TILE, tn=512, out_dtype=F32, n_used=n_used)
    return moe_combine(yb, pos0, pos1, slab)


def _row_tile(k):
    return 1024 if k <= 2048 else 512


def _dense_ids(n_rows, tm, idx):
    return jnp.full((n_rows // tm,), idx, jnp.int32)


def _col_tile(k, tm, n_out, first_col, out_bytes):
    budget = VMEM_LIMIT - 12 * 1024 * 1024
    for tn in (1024, 768, 512, 256):
        if n_out % tn or first_col % tn:
            continue
        working = k * tn * (4 + 2) + 2 * tm * k * 2 + 2 * tm * tn * out_bytes + tm * tn * 4
        if working <= budget:
            return tn
    raise ValueError("no column tile fits")


def _dense(a3, w, idx, *, bias=None, out_dtype=BF16, first_col=0, n_out=None):
    parts = tuple(a3) if isinstance(a3, (tuple, list)) else (a3,)
    b, s, _ = parts[0].shape
    k = w.shape[1]
    m = b * s
    tm = min(_row_tile(k), m)
    n_out = w.shape[2] if n_out is None else n_out
    tn = _col_tile(k, tm, n_out, first_col, jnp.dtype(out_dtype).itemsize)
    out = gmm(tuple(p.reshape(m, p.shape[2]) for p in parts), w, _dense_ids(m, tm, idx), bias, tm=tm, tn=tn,
              out_dtype=out_dtype, col_off=first_col // tn, n_out=n_out)
    return out.reshape(b, s, -1)


def _dense_swiglu(h3, w1, w3, w2, idx):
    b, s, k = h3.shape
    m = b * s
    tm = min(1024, m)
    mid = swiglu_up(h3.reshape(m, k), w1, w3, _dense_ids(m, tm, idx), tm=tm, tn=512)
    tm2 = min(_row_tile(mid.shape[1]), m)
    return gmm(mid, w2, _dense_ids(m, tm2, idx), tm=tm2, tn=512, out_dtype=BF16).reshape(b, s, -1)


def _attention_layer(h, hc, i, w_in, w_out, q_g, k_g, bias, cos2, sin2, ctx_out):
    d = h.shape[-1]
    n_na = d // (2 * HEAD_DIM)
    n_q = n_na
    n_kv = max(1, n_q // 4)
    group = n_q // n_kv
    c_bq, c_ak, c_av = n_na, n_na + n_q, 2 * n_na + n_q
    c_bk, c_bv = 3 * n_na + n_q, 3 * n_na + n_q + n_kv
    q_cols = (n_na + n_q) * HEAD_DIM
    proj = _dense(h, w_in, i)
    if ctx_out:
        projc = _dense(hc, w_in, i)
        shift = 0
    else:
        projc = _dense(hc, w_in, i, first_col=q_cols, n_out=w_in.shape[2] - q_cols)
        shift = n_na + n_q
    qb = head_prep(proj, c_bq, n_q, q_g, cos2, sin2, rope=True)
    kb = head_prep(proj, c_bk, n_kv, k_g, cos2, sin2, rope=True)
    kbc = head_prep(projc, c_bk - shift, n_kv, k_g, cos2, sin2, rope=False)
    o_a = na_attention(proj, projc, n_na, 0, c_ak, c_av, c_ak - shift, c_av - shift, bias)
    o_b = full_attention(qb, 0, n_kv, group, [(kbc, 0), (kb, 0)], [(projc, c_bv - shift), (proj, c_bv)])
    y = _dense((o_a, o_b), w_out, i)
    if not ctx_out:
        return y, None
    qbc = head_prep(projc, c_bq, n_q, q_g, cos2, sin2, rope=False)
    oc_a = full_attention(projc, 0, n_na, 1, [(projc, c_ak)], [(projc, c_av)])
    oc_b = full_attention(qbc, 0, n_kv, group, [(kbc, 0)], [(projc, c_bv)])
    yc = _dense((oc_a, oc_b), w_out, i)
    return y, yc


def _hyena_layer(h, i, p):
    (w_in, b_in, conv_w, conv_b, f_w1, f_b1, f_w2, f_b2, f_w3, freq, skip, w_out, b_out) = p
    n_tok = h.shape[1]
    c = min(HY_CHUNK, n_tok)
    u0 = _dense(h, w_in, i, bias=b_in[i].reshape(1, -1))
    u = short_conv(u0, conv_w[i], conv_b[i])
    fwd_u, fwd_g, inv = dft_matrices(c)
    klag = hyena_lag_filters(n_tok, f_w1[i], f_b1[i], f_w2[i], f_b2[i], f_w3[i], freq[i])
    spectra = filter_spectra(klag, fwd_g, c)
    z = long_conv_gate(u, 0, u, 1, spectra, 0, skip[i], fwd_u, inv, c)
    z = long_conv_gate(z, 0, u, 2, spectra, 1, skip[i], fwd_u, inv, c)
    return _dense(z, w_out, i, bias=b_out[i].reshape(1, -1))


def kernel(x, c, ctx, c_ctx, mod_w, mod_b, norm_g, att_w_in, att_w_out, att_q_norm_g, att_k_norm_g, na_rpb,
           ffn_w1, ffn_w3, ffn_w2, hy_w_in, hy_b_in, hy_conv_w, hy_conv_b, hy_f_w1, hy_f_b1, hy_f_w2, hy_f_b2,
           hy_f_w3, hy_freq, hy_skip, hy_w_out, hy_b_out, moe_router, moe_w1, moe_w3, moe_w2):
    depth = mod_w.shape[0]
    b, s, d = x.shape
    n_ctx = ctx.shape[1]
    d_ff = moe_w1.shape[-1]
    cos2, sin2 = rope_tables(s)
    moe_w1f = moe_w1.reshape(-1, d, d_ff)
    moe_w3f = moe_w3.reshape(-1, d, d_ff)
    moe_w2f = moe_w2.reshape(-1, d_ff, d)
    hy = (hy_w_in, hy_b_in, hy_conv_w, hy_conv_b, hy_f_w1, hy_f_b1, hy_f_w2, hy_f_b2, hy_f_w3, hy_freq,
          hy_skip, hy_w_out, hy_b_out)

    n_mod_rows = 16
    cond = jnp.concatenate([c, c_ctx[None, :], jnp.zeros((n_mod_rows - b - 1, d), F32)], axis=0)

    lat_mods, ctx_mods = [], []
    for l in range(depth):
        mod = gmm(cond, mod_w, _dense_ids(n_mod_rows, n_mod_rows, l), mod_b[l].reshape(1, -1),
                  tm=n_mod_rows, tn=1024, out_dtype=F32, silu_in=True)
        m_lat = mod[:b].reshape(b, 6, 1, d)
        lat_mods.append([m_lat[:, k] for k in range(6)])
        m_ctx = jnp.broadcast_to(mod[b].reshape(1, 6, 1, d), (b, 6, 1, d))
        ctx_mods.append([m_ctx[:, k] for k in range(6)])

    xc = ctx
    h = modulate(x, norm_g[0, 0], lat_mods[0][0], lat_mods[0][1], out_dtype=BF16)
    for l in range(depth):
        even = l % 2 == 0
        i = l // 2
        ctx_out = any(m % 2 == 0 for m in range(l + 1, depth))
        ctx_in = even or ctx_out
        sh1, sc1, g1, sh2, sc2, g2 = lat_mods[l]
        csh1, csc1, cg1, csh2, csc2, cg2 = ctx_mods[l]

        hc = modulate(xc, norm_g[l, 0], csh1, csc1, out_dtype=BF16) if ctx_in else None
        if even:
            bias = na_bias_blocks(na_rpb[i])
            y, yc = _attention_layer(h, hc, i, att_w_in, att_w_out, att_q_norm_g[i], att_k_norm_g[i], bias,
                                     cos2, sin2, ctx_out)
        else:
            y = _hyena_layer(h, i, hy)
            yc = _hyena_layer(hc, i, hy) if ctx_out else None
        ffn_in = BF16 if even else F32
        x, h = residual_modulate(x, y, norm_g[l, 1], g1, norm_g[l, 2], sh2, sc2, out_dtype=ffn_in)
        if ctx_out:
            xc = residual(xc, yc, norm_g[l, 1], cg1)
            hc = modulate(xc, norm_g[l, 2], csh2, csc2, out_dtype=ffn_in)
        y_off = yc_off = 0
        if even:
            y = _dense_swiglu(h, ffn_w1, ffn_w3, ffn_w2, i)
            yc = _dense_swiglu(hc, ffn_w1, ffn_w3, ffn_w2, i) if ctx_out else None
        elif ctx_out and (b * n_ctx) % s == 0:
            tokens = jnp.concatenate([h.reshape(b * s, d), hc.reshape(b * n_ctx, d)], axis=0)
            y_all = moe_swiglu(tokens, moe_router[i], moe_w1f, moe_w3f, moe_w2f, i)
            y = y_all.reshape(-1, s, d)
            yc = y_all.reshape(-1, n_ctx, d)
            yc_off = b * s // n_ctx
        else:
            y = moe_swiglu(h.reshape(b * s, d), moe_router[i], moe_w1f, moe_w3f, moe_w2f, i).reshape(b, s, d)
            yc = None
            if ctx_out:
                yc = moe_swiglu(hc.reshape(b * n_ctx, d), moe_router[i], moe_w1f, moe_w3f, moe_w2f,
                                i).reshape(b, n_ctx, d)
        if l + 1 < depth:
            x, h = residual_modulate(x, y, norm_g[l, 3], g2, norm_g[l + 1, 0], lat_mods[l + 1][0],
                                     lat_mods[l + 1][1], out_dtype=BF16, y_off=y_off)
        else:
            x = residual(x, y, norm_g[l, 3], g2, y_off=y_off)
        if ctx_out:
            xc = residual(xc, yc, norm_g[l, 3], cg2, y_off=yc_off)
    return x
```

```python
import functools
import math

import jax
import jax.numpy as jnp
import numpy as np
from jax import lax
from jax.experimental import pallas as pl
from jax.experimental.pallas import tpu as pltpu

F32 = jnp.float32
BF16 = jnp.bfloat16
HIGHEST = lax.Precision.HIGHEST

GRID_W = 64
HEAD_DIM = 128
NA_WIN_ROWS = 8
NA_WIN_COLS = 16
ROPE_BASE = 10000.0
HY_ORDER = 2
HY_SHORT = 3
HY_EMB_DIM = 33
HY_TARGET = 1e-2
HY_FAST_DECAY = 0.3
HY_SLOW_DECAY = 1.5
HY_MOD_SHIFT = 0.05
N_EXPERTS = 8
NORM_EPS = 1e-6
NEG_INF = -1e30

LANES = 128
V7X_VMEM_BYTES = 64 * 1024 * 1024
VMEM_LIMIT = 56 * 1024 * 1024

ROW_TILE = 512
MOE_TILE = 512
GATHER_UNROLL = 8
HY_CHUNK = 512
HY_SLAB = 32
NA_QROWS = 4
NA_KROWS = 12


def _cparams(sem):
    return pltpu.CompilerParams(dimension_semantics=sem, vmem_limit_bytes=VMEM_LIMIT)


def _weights_changed(be_ref, i):
    prev = be_ref[jnp.maximum(i - 1, 0)]
    return (i == 0) | (be_ref[i] != prev)


def _next_run_expert(block_e):
    nb = block_e.shape[0]
    idx = jnp.arange(nb, dtype=jnp.int32)
    run_last = jnp.concatenate([block_e[1:] != block_e[:-1], jnp.ones((1,), bool)])
    run_end = lax.cummin(jnp.where(run_last, idx, nb), axis=0, reverse=True)
    nxt = run_end + 1
    return jnp.where(nxt < nb, block_e[jnp.minimum(nxt, nb - 1)], -1).astype(jnp.int32)


def _stream_weights(be_ref, nxt_ref, bufs, sems, tn, col_off):
    j = pl.program_id(0)
    i = pl.program_id(1)
    n_col = pl.num_programs(0)

    def tile_copies(e, jj):
        cols = pl.ds(pl.multiple_of((jj + col_off) * tn, tn), tn)
        return [pltpu.make_async_copy(w.at[e, :, cols], stage, sems.at[n])
                for n, (w, stage, _) in enumerate(bufs)]

    @pl.when((j == 0) & (i == 0))
    def _():
        for cp in tile_copies(be_ref[0], 0):
            cp.start()

    @pl.when(_weights_changed(be_ref, i))
    def _():
        for cp, (_, stage, wbf) in zip(tile_copies(be_ref[i], j), bufs):
            cp.wait()
            wbf[...] = stage[...].astype(BF16)
        next_e = nxt_ref[i]

        @pl.when(next_e >= 0)
        def _():
            for cp in tile_copies(next_e, j):
                cp.start()

        @pl.when((next_e < 0) & (j + 1 < n_col))
        def _():
            for cp in tile_copies(be_ref[0], j + 1):
                cp.start()


def _gmm_kernel(be_ref, nu_ref, nxt_ref, *refs, n_a, silu_in, tn, col_off):
    a_refs = refs[:n_a]
    w_hbm, b_ref, o_ref, stage_ref, wbf_ref, sems = refs[n_a:]
    i = pl.program_id(1)
    used = i < nu_ref[0]
    _stream_weights(be_ref, nxt_ref, [(w_hbm, stage_ref, wbf_ref)], sems, tn, col_off)

    @pl.when(used)
    def _():
        acc = b_ref[...]
        k0 = 0
        for a_ref in a_refs:
            a = a_ref[...]
            if silu_in:
                a = a.astype(F32)
                a = a * jax.nn.sigmoid(a)
            k1 = k0 + a.shape[1]
            acc = acc + jnp.dot(a.astype(BF16), wbf_ref[k0:k1, :], preferred_element_type=F32)
            k0 = k1
        o_ref[...] = acc.astype(o_ref.dtype)

    @pl.when(jnp.logical_not(used))
    def _():
        o_ref[...] = jnp.zeros_like(o_ref)


def _all_used(block_e):
    return jnp.full((1,), block_e.shape[0], jnp.int32)


def gmm(a, w, block_e, bias=None, *, tm, tn, out_dtype, col_off=0, n_out=None, silu_in=False, n_used=None):
    a_parts = tuple(a) if isinstance(a, (tuple, list)) else (a,)
    m = a_parts[0].shape[0]
    k = sum(p.shape[1] for p in a_parts)
    assert k == w.shape[1]
    n_out = w.shape[2] if n_out is None else n_out
    if bias is None:
        bias = jnp.zeros((1, w.shape[2]), F32)
    n_used = _all_used(block_e) if n_used is None else n_used
    grid = (n_out // tn, m // tm)
    a_specs = [pl.BlockSpec((tm, p.shape[1]), lambda j, i, be, nu, nx: (jnp.minimum(i, nu[0] - 1), 0))
               for p in a_parts]
    return pl.pallas_call(
        functools.partial(_gmm_kernel, n_a=len(a_parts), silu_in=silu_in, tn=tn, col_off=col_off),
        out_shape=jax.ShapeDtypeStruct((m, n_out), out_dtype),
        grid_spec=pltpu.PrefetchScalarGridSpec(
            num_scalar_prefetch=3,
            grid=grid,
            in_specs=a_specs + [
                pl.BlockSpec(memory_space=pl.ANY),
                pl.BlockSpec((1, tn), lambda j, i, be, nu, nx: (0, j + col_off)),
            ],
            out_specs=pl.BlockSpec((tm, tn), lambda j, i, be, nu, nx: (i, j)),
            scratch_shapes=[pltpu.VMEM((k, tn), F32), pltpu.VMEM((k, tn), BF16),
                            pltpu.SemaphoreType.DMA((1,))],
        ),
        compiler_params=_cparams(("arbitrary", "arbitrary")),
        name="gmm",
    )(block_e, n_used, _next_run_expert(block_e), *a_parts, w, bias)


def _up_kernel(be_ref, nu_ref, nxt_ref, a_ref, w1_hbm, w3_hbm, o_ref, stage1_ref, stage3_ref, w1bf_ref,
               w3bf_ref, sems, *, tn):
    i = pl.program_id(1)
    used = i < nu_ref[0]
    _stream_weights(be_ref, nxt_ref, [(w1_hbm, stage1_ref, w1bf_ref), (w3_hbm, stage3_ref, w3bf_ref)],
                    sems, tn, 0)

    @pl.when(used)
    def _():
        a = a_ref[...]
        g = jnp.dot(a, w1bf_ref[...], preferred_element_type=F32)
        u = jnp.dot(a, w3bf_ref[...], preferred_element_type=F32)
        o_ref[...] = (g * jax.nn.sigmoid(g) * u).astype(o_ref.dtype)

    @pl.when(jnp.logical_not(used))
    def _():
        o_ref[...] = jnp.zeros_like(o_ref)


def swiglu_up(a, w1, w3, block_e, *, tm, tn, n_used=None):
    m, k = a.shape
    n = w1.shape[2]
    n_used = _all_used(block_e) if n_used is None else n_used
    grid = (n // tn, m // tm)
    return pl.pallas_call(
        functools.partial(_up_kernel, tn=tn),
        out_shape=jax.ShapeDtypeStruct((m, n), BF16),
        grid_spec=pltpu.PrefetchScalarGridSpec(
            num_scalar_prefetch=3,
            grid=grid,
            in_specs=[
                pl.BlockSpec((tm, k), lambda j, i, be, nu, nx: (jnp.minimum(i, nu[0] - 1), 0)),
                pl.BlockSpec(memory_space=pl.ANY),
                pl.BlockSpec(memory_space=pl.ANY),
            ],
            out_specs=pl.BlockSpec((tm, tn), lambda j, i, be, nu, nx: (i, j)),
            scratch_shapes=[pltpu.VMEM((k, tn), F32), pltpu.VMEM((k, tn), F32),
                            pltpu.VMEM((k, tn), BF16), pltpu.VMEM((k, tn), BF16),
                            pltpu.SemaphoreType.DMA((2,))],
        ),
        compiler_params=_cparams(("arbitrary", "arbitrary")),
        name="swiglu_up",
    )(block_e, n_used, _next_run_expert(block_e), a, w1, w3)


def _rms(x, g):
    ms = jnp.mean(x * x, axis=-1, keepdims=True)
    return x * lax.rsqrt(ms + NORM_EPS) * g


def _mod_kernel(x_ref, g_ref, sh_ref, sc_ref, o_ref):
    y = _rms(x_ref[0], g_ref[...])
    o_ref[0] = (y * (1.0 + sc_ref[0]) + sh_ref[0]).astype(o_ref.dtype)


def modulate(x, g, shift, scale, *, out_dtype, tr=ROW_TILE):
    b, s, d = x.shape
    tr = min(tr, s)
    row = pl.BlockSpec((1, tr, d), lambda bi, si: (bi, si, 0))
    vec = pl.BlockSpec((1, 1, d), lambda bi, si: (bi, 0, 0))
    return pl.pallas_call(
        _mod_kernel,
        out_shape=jax.ShapeDtypeStruct((b, s, d), out_dtype),
        grid=(b, s // tr),
        in_specs=[row, pl.BlockSpec((1, d), lambda bi, si: (0, 0)), vec, vec],
        out_specs=row,
        compiler_params=_cparams(("parallel", "parallel")),
        name="modulate",
    )(x, g.reshape(1, d), shift, scale)


def _resid_kernel(x_ref, y_ref, g_ref, gate_ref, o_ref):
    y = _rms(y_ref[0].astype(F32), g_ref[...])
    o_ref[0] = x_ref[0] + gate_ref[0] * y


def residual(x, y, g, gate, *, tr=ROW_TILE, y_off=0):
    b, s, d = x.shape
    tr = min(tr, s)
    row = pl.BlockSpec((1, tr, d), lambda bi, si: (bi, si, 0))
    y_row = pl.BlockSpec((1, tr, d), lambda bi, si: (bi + y_off, si, 0))
    vec = pl.BlockSpec((1, 1, d), lambda bi, si: (bi, 0, 0))
    return pl.pallas_call(
        _resid_kernel,
        out_shape=jax.ShapeDtypeStruct((b, s, d), F32),
        grid=(b, s // tr),
        in_specs=[row, y_row, pl.BlockSpec((1, d), lambda bi, si: (0, 0)), vec],
        out_specs=row,
        compiler_params=_cparams(("parallel", "parallel")),
        name="residual",
    )(x, y, g.reshape(1, d), gate)


def _resid_mod_kernel(x_ref, y_ref, gy_ref, gate_ref, gh_ref, sh_ref, sc_ref, xo_ref, ho_ref):
    x = x_ref[0] + gate_ref[0] * _rms(y_ref[0].astype(F32), gy_ref[...])
    xo_ref[0] = x
    ho_ref[0] = (_rms(x, gh_ref[...]) * (1.0 + sc_ref[0]) + sh_ref[0]).astype(ho_ref.dtype)


def residual_modulate(x, y, g_y, gate, g_h, shift, scale, *, out_dtype, tr=ROW_TILE, y_off=0):
    b, s, d = x.shape
    tr = min(tr, s)
    row = pl.BlockSpec((1, tr, d), lambda bi, si: (bi, si, 0))
    y_row = pl.BlockSpec((1, tr, d), lambda bi, si: (bi + y_off, si, 0))
    vec = pl.BlockSpec((1, 1, d), lambda bi, si: (bi, 0, 0))
    gain = pl.BlockSpec((1, d), lambda bi, si: (0, 0))
    return pl.pallas_call(
        _resid_mod_kernel,
        out_shape=(jax.ShapeDtypeStruct((b, s, d), F32), jax.ShapeDtypeStruct((b, s, d), out_dtype)),
        grid=(b, s // tr),
        in_specs=[row, y_row, gain, vec, gain, vec, vec],
        out_specs=(row, row),
        compiler_params=_cparams(("parallel", "parallel")),
        name="residual_modulate",
    )(x, y, g_y.reshape(1, d), gate, g_h.reshape(1, d), shift, scale)


def _prep_kernel(x_ref, g_ref, cos_ref, sin_ref, o_ref, *, rope, n_heads):
    for h in range(n_heads):
        cols = slice(h * HEAD_DIM, (h + 1) * HEAD_DIM)
        y = _rms(x_ref[0, :, cols].astype(F32), g_ref[...])
        if rope:
            rot = pltpu.roll(y, HEAD_DIM // 2, axis=1)
            y = y * cos_ref[...] + rot * sin_ref[...]
        o_ref[0, :, cols] = y.astype(o_ref.dtype)


def head_prep(src, col0, n_heads, g, cos2, sin2, *, rope):
    b, s, _ = src.shape
    ts = min(s, 512)
    width = n_heads * HEAD_DIM
    assert col0 % n_heads == 0
    return pl.pallas_call(
        functools.partial(_prep_kernel, rope=rope, n_heads=n_heads),
        out_shape=jax.ShapeDtypeStruct((b, s, width), BF16),
        grid=(b, s // ts),
        in_specs=[
            pl.BlockSpec((1, ts, width), lambda bi, si: (bi, si, col0 // n_heads)),
            pl.BlockSpec((1, HEAD_DIM), lambda bi, si: (0, 0)),
            pl.BlockSpec((ts, HEAD_DIM), lambda bi, si: (si, 0)),
            pl.BlockSpec((ts, HEAD_DIM), lambda bi, si: (si, 0)),
        ],
        out_specs=pl.BlockSpec((1, ts, width), lambda bi, si: (bi, si, 0)),
        compiler_params=_cparams(("parallel", "parallel")),
        name="head_prep",
    )(src, g.reshape(1, HEAD_DIM), cos2[:s], sin2[:s])


def _nt_dot(q, k):
    return lax.dot_general(q, k, (((1,), (1,)), ((), ())), preferred_element_type=F32)


def _with_ones(v):
    return jnp.concatenate([v, jnp.ones(v.shape, v.dtype)], axis=1)


def _softmax_pv(scores, values_ext, out_dtype):
    m = scores[0].max(axis=-1, keepdims=True)
    for s in scores[1:]:
        m = jnp.maximum(m, s.max(axis=-1, keepdims=True))
    ps = [jnp.exp((s - m).astype(BF16)) for s in scores]
    acc = jnp.dot(ps[0], values_ext[0], preferred_element_type=F32)
    for p, v in zip(ps[1:], values_ext[1:]):
        acc = acc + jnp.dot(p, v, preferred_element_type=F32)
    return (acc[:, :HEAD_DIM] / acc[:, HEAD_DIM:HEAD_DIM + 1]).astype(out_dtype)


def _scaled(q):
    return (q.astype(F32) * (HEAD_DIM ** -0.5)).astype(BF16)


def _attn_kernel(*refs, group, n_parts):
    q_ref = refs[0]
    k_refs = refs[1:1 + n_parts]
    v_refs = refs[1 + n_parts:1 + 2 * n_parts]
    o_ref = refs[1 + 2 * n_parts]
    ks = [r[0] for r in k_refs]
    vs = [_with_ones(r[0]) for r in v_refs]
    for g in range(group):
        cols = slice(g * HEAD_DIM, (g + 1) * HEAD_DIM)
        q = _scaled(q_ref[0, :, cols])
        scores = [_nt_dot(q, k) for k in ks]
        o_ref[0, :, cols] = _softmax_pv(scores, vs, o_ref.dtype)


def full_attention(q_src, q_col0, n_kv, group, k_parts, v_parts, *, tq=512):
    b, sq, _ = q_src.shape
    tq = min(tq, sq)
    gw = group * HEAD_DIM
    assert (q_col0 * HEAD_DIM) % gw == 0
    q_blk0 = q_col0 * HEAD_DIM // gw
    in_specs = [pl.BlockSpec((1, tq, gw), lambda bi, h, qi: (bi, qi, q_blk0 + h))]
    args = [q_src]
    for arr, c0 in list(k_parts) + list(v_parts):
        in_specs.append(pl.BlockSpec((1, arr.shape[1], HEAD_DIM),
                                     functools.partial(lambda bi, h, qi, c0: (bi, 0, c0 + h), c0=c0)))
        args.append(arr)
    return pl.pallas_call(
        functools.partial(_attn_kernel, group=group, n_parts=len(k_parts)),
        out_shape=jax.ShapeDtypeStruct((b, sq, n_kv * gw), BF16),
        grid=(b, n_kv, sq // tq),
        in_specs=in_specs,
        out_specs=pl.BlockSpec((1, tq, gw), lambda bi, h, qi: (bi, qi, h)),
        compiler_params=_cparams(("parallel", "parallel", "parallel")),
        name="full_attention",
    )(*args)


NA_MASKED = 2 * NA_WIN_ROWS - 1


def _na_bias(bias_ref, r0, start, rows):
    lane = lax.broadcasted_iota(jnp.int32, (GRID_W, 2 * GRID_W), 1)
    bands = []
    for i in range(NA_QROWS):
        r = r0 + i
        rs = min(max(r - NA_WIN_ROWS // 2, 0), rows - NA_WIN_ROWS)

        def block_id(j):
            kr = start + j
            return kr - r + NA_WIN_ROWS - 1 if rs <= kr < rs + NA_WIN_ROWS else NA_MASKED

        tiles = []
        for m in range(NA_KROWS // 2):
            d0, d1 = block_id(2 * m), block_id(2 * m + 1)
            tile = bias_ref[0, d0]
            if d1 != d0:
                tile = jnp.where(lane < GRID_W, tile, bias_ref[0, d1])
            tiles.append(tile)
        bands.append(jnp.concatenate(tiles, axis=1))
    return jnp.concatenate(bands, axis=0)


def _na_kernel(q_ref, k_ref, v_ref, kc_ref, vc_ref, bias_ref, o_ref, *, rows):
    n_steps = rows // NA_QROWS
    kc = kc_ref[0]
    vc = _with_ones(vc_ref[0])
    for step in range(n_steps):
        r0 = step * NA_QROWS
        start = min(max(r0 - NA_WIN_ROWS // 2, 0), rows - NA_KROWS)
        qs = slice(r0 * GRID_W, (r0 + NA_QROWS) * GRID_W)
        kslice = slice(start * GRID_W, (start + NA_KROWS) * GRID_W)
        q = _scaled(q_ref[0, qs, :])
        s_lat = _nt_dot(q, k_ref[0, kslice, :]) + _na_bias(bias_ref, r0, start, rows)
        s_ctx = _nt_dot(q, kc)
        o_ref[0, qs, :] = _softmax_pv([s_lat, s_ctx], [_with_ones(v_ref[0, kslice, :]), vc], o_ref.dtype)


def na_bias_blocks(rpb):
    n_dc = 2 * NA_WIN_COLS - 1
    col = np.arange(GRID_W)
    col_start = np.clip(col - NA_WIN_COLS // 2, 0, GRID_W - NA_WIN_COLS)
    col_ok = (col[None, :] >= col_start[:, None]) & (col[None, :] < col_start[:, None] + NA_WIN_COLS)
    dc = np.clip(col[None, :] - col[:, None] + NA_WIN_COLS - 1, 0, n_dc - 1)
    pick_dc = (dc[:, :, None] == np.arange(n_dc)).astype(np.float32)
    blocks = jnp.einsum('hde,qke->hdqk', rpb.astype(F32), pick_dc, precision=HIGHEST)
    blocks = jnp.where(col_ok[None, None], blocks, NEG_INF)
    masked = jnp.full((rpb.shape[0], 1, GRID_W, GRID_W), NEG_INF, F32)
    blocks = jnp.concatenate([blocks, masked], axis=1)
    return jnp.concatenate([blocks, blocks], axis=-1)


def na_attention(proj, projc, n_heads, q_col0, k_col0, v_col0, kc_col0, vc_col0, bias):
    b, s, _ = proj.shape
    n_ctx = projc.shape[1]
    rows = s // GRID_W
    assert rows % NA_QROWS == 0 and NA_KROWS % 2 == 0 and 2 * GRID_W == LANES
    lat = lambda c0: pl.BlockSpec((1, s, HEAD_DIM), lambda h, bi: (bi, 0, c0 + h))
    ctx = lambda c0: pl.BlockSpec((1, n_ctx, HEAD_DIM), lambda h, bi: (bi, 0, c0 + h))
    return pl.pallas_call(
        functools.partial(_na_kernel, rows=rows),
        out_shape=jax.ShapeDtypeStruct((b, s, n_heads * HEAD_DIM), BF16),
        grid=(n_heads, b),
        in_specs=[lat(q_col0), lat(k_col0), lat(v_col0), ctx(kc_col0), ctx(vc_col0),
                  pl.BlockSpec((1,) + bias.shape[1:], lambda h, bi: (h, 0, 0, 0))],
        out_specs=pl.BlockSpec((1, s, HEAD_DIM), lambda h, bi: (bi, 0, h)),
        compiler_params=_cparams(("parallel", "parallel")),
        name="na_attention",
    )(proj, proj, proj, projc, projc, bias)


def rope_tables(n_tok):
    t = jnp.arange(n_tok)
    row = (t // GRID_W).astype(F32)
    col = (t % GRID_W).astype(F32)
    n_freq = HEAD_DIM // 4
    inv_freq = ROPE_BASE ** (-jnp.arange(n_freq, dtype=F32) / n_freq)
    ang = jnp.concatenate([row[:, None] * inv_freq, col[:, None] * inv_freq], axis=-1)
    cos, sin = jnp.cos(ang), jnp.sin(ang)
    return jnp.concatenate([cos, cos], axis=-1), jnp.concatenate([-sin, sin], axis=-1)


def _shortconv_kernel(u_ref, w_ref, b_ref, o_ref):
    u = u_ref[0].astype(F32)
    n = u.shape[0]
    row = lax.broadcasted_iota(jnp.int32, u.shape, 0)
    prev = jnp.where(row == 0, 0.0, pltpu.roll(u, 1, axis=0))
    nxt = jnp.where(row == n - 1, 0.0, pltpu.roll(u, n - 1, axis=0))
    o = prev * w_ref[0:1, :] + u * w_ref[1:2, :] + nxt * w_ref[2:3, :] + b_ref[...]
    o_ref[0] = o.astype(o_ref.dtype)


def short_conv(u, w, bias, *, tc=256):
    b, n, c = u.shape
    return pl.pallas_call(
        _shortconv_kernel,
        out_shape=jax.ShapeDtypeStruct((b, n, c), BF16),
        grid=(b, c // tc),
        in_specs=[pl.BlockSpec((1, n, tc), lambda bi, j: (bi, 0, j)),
                  pl.BlockSpec((HY_SHORT, tc), lambda bi, j: (0, j)),
                  pl.BlockSpec((1, tc), lambda bi, j: (0, j))],
        out_specs=pl.BlockSpec((1, n, tc), lambda bi, j: (bi, 0, j)),
        compiler_params=_cparams(("parallel", "parallel")),
        name="short_conv",
    )(u, w, bias.reshape(1, c))


def _filter_mlp_kernel(z_ref, w1_ref, b1_ref, w2_ref, b2_ref, fr_ref, o_ref):
    fr = fr_ref[...]
    a = jnp.sin(fr * (jnp.dot(z_ref[...], w1_ref[...], precision=HIGHEST, preferred_element_type=F32) + b1_ref[...]))
    o_ref[...] = jnp.sin(fr * (jnp.dot(a, w2_ref[...], precision=HIGHEST, preferred_element_type=F32) + b2_ref[...]))


def _split_bf16(x):
    hi = x.astype(BF16)
    return hi, (x - hi.astype(F32)).astype(BF16)


def _dot3(a, w):
    a_hi, a_lo = _split_bf16(a)
    w_hi, w_lo = _split_bf16(w)
    return (jnp.dot(a_hi, w_hi, preferred_element_type=F32) + jnp.dot(a_lo, w_hi, preferred_element_type=F32)
            + jnp.dot(a_hi, w_lo, preferred_element_type=F32))


def _filter_kernel(a_ref, t_ref, w3f_ref, w3b_ref, dl_ref, o_ref, *, n_tok):
    a = a_ref[...]
    f_fwd = _dot3(a, w3f_ref[...])
    f_bwd = _dot3(a, w3b_ref[...])
    window = jnp.exp(-t_ref[...] * dl_ref[...]) + HY_MOD_SHIFT
    lag_row = lax.broadcasted_iota(jnp.int32, f_fwd.shape, 0)
    filt = jnp.where(lag_row >= n_tok, f_fwd, f_bwd) * window
    o_ref[0] = jnp.where(lag_row == 0, 0.0, filt)


def hyena_lag_filters(n_tok, f_w1, f_b1, f_w2, f_b2, f_w3, freq, *, tc=256):
    d = f_w3.shape[1] // (2 * HY_ORDER)
    hidden = f_w1.shape[1]
    bands = (HY_EMB_DIM - 1) // 2
    pos = jnp.abs(jnp.arange(2 * n_tok) - n_tok)
    pos = jnp.minimum(pos, n_tok - 1)
    t = jnp.linspace(0.0, 1.0, n_tok, dtype=F32)[pos][:, None]
    w = (2.0 * math.pi * jnp.arange(n_tok, dtype=F32) / n_tok)[pos][:, None]
    f = jnp.linspace(1e-4, bands - 1, bands, dtype=F32)[None, :]
    z = jnp.concatenate([t, jnp.cos(f * w), -jnp.sin(f * w)], axis=-1)
    z = jnp.pad(z, ((0, 0), (0, LANES - HY_EMB_DIM)))
    w1 = jnp.pad(f_w1.astype(F32), ((0, LANES - HY_EMB_DIM), (0, 0)))
    max_decay = math.log(HY_TARGET) / HY_FAST_DECAY
    min_decay = math.log(HY_TARGET) / HY_SLOW_DECAY
    deltas = jnp.abs(jnp.linspace(min_decay, max_decay, d, dtype=F32)).reshape(1, d)
    nj = d // tc
    n_lag = 2 * n_tok
    one = lambda shape: pl.BlockSpec(shape, lambda i: (0,) * len(shape))
    feats = pl.pallas_call(
        _filter_mlp_kernel,
        out_shape=jax.ShapeDtypeStruct((n_lag, hidden), F32),
        grid=(1,),
        in_specs=[one((n_lag, LANES)), one((LANES, hidden)), one((1, hidden)), one((hidden, hidden)),
                  one((1, hidden)), one((1, hidden))],
        out_specs=one((n_lag, hidden)),
        compiler_params=_cparams(("arbitrary",)),
        name="hyena_filter_mlp",
    )(z, w1, f_b1.reshape(1, hidden).astype(F32), f_w2.astype(F32), f_b2.reshape(1, hidden).astype(F32),
      freq.reshape(1, hidden).astype(F32))
    full = lambda shape: pl.BlockSpec(shape, lambda o, j: (0,) * len(shape))
    return pl.pallas_call(
        functools.partial(_filter_kernel, n_tok=n_tok),
        out_shape=jax.ShapeDtypeStruct((HY_ORDER, n_lag, d), F32),
        grid=(HY_ORDER, nj),
        in_specs=[full((n_lag, hidden)), full((n_lag, 1)),
                  pl.BlockSpec((hidden, tc), lambda o, j: (0, (2 * o) * nj + j)),
                  pl.BlockSpec((hidden, tc), lambda o, j: (0, (2 * o + 1) * nj + j)),
                  pl.BlockSpec((1, tc), lambda o, j: (0, j))],
        out_specs=pl.BlockSpec((1, n_lag, tc), lambda o, j: (o, 0, j)),
        compiler_params=_cparams(("parallel", "parallel")),
        name="hyena_filters",
    )(feats, t, f_w3.astype(F32), f_w3.astype(F32), deltas)


def _dft_angles(f, n, c):
    k = jnp.mod((2 * f + 1) * n, 4 * c).astype(F32)
    return k * (2.0 * math.pi / (4 * c))


def dft_matrices(c):
    f = jnp.arange(c)[:, None]
    ang_u = _dft_angles(f, jnp.arange(c)[None, :], c)
    fwd_u = jnp.concatenate([jnp.cos(ang_u), -jnp.sin(ang_u)], axis=0)
    ang_g = _dft_angles(f, jnp.arange(-c, c)[None, :], c)
    fwd_g = jnp.concatenate([jnp.cos(ang_g), -jnp.sin(ang_g)], axis=0)
    inv = jnp.concatenate([jnp.cos(ang_u).T, -jnp.sin(ang_u).T], axis=1) / c
    return fwd_u.astype(BF16), fwd_g.astype(BF16), inv.astype(BF16)


def _spectra_kernel(fa_ref, fb_ref, ka_ref, kb_ref, o_ref):
    acc = jnp.dot(fa_ref[...], ka_ref[0].astype(BF16), preferred_element_type=F32)
    acc = acc + jnp.dot(fb_ref[...], kb_ref[0].astype(BF16), preferred_element_type=F32)
    o_ref[0, 0] = acc.astype(o_ref.dtype)


def filter_spectra(klag, fwd_g, c, *, tc=1024):
    n_ord, two_l, d = klag.shape
    n_delta = two_l // c - 1
    fa, fb = fwd_g[:, :c], fwd_g[:, c:]
    return pl.pallas_call(
        _spectra_kernel,
        out_shape=jax.ShapeDtypeStruct((n_ord, n_delta, 2 * c, d), BF16),
        grid=(n_ord, n_delta, d // tc),
        in_specs=[pl.BlockSpec((2 * c, c), lambda o, dl, j: (0, 0)),
                  pl.BlockSpec((2 * c, c), lambda o, dl, j: (0, 0)),
                  pl.BlockSpec((1, c, tc), lambda o, dl, j: (o, dl, j)),
                  pl.BlockSpec((1, c, tc), lambda o, dl, j: (o, dl + 1, j))],
        out_specs=pl.BlockSpec((1, 1, 2 * c, tc), lambda o, dl, j: (o, dl, 0, j)),
        compiler_params=_cparams(("parallel", "parallel", "parallel")),
        name="filter_spectra",
    )(fa, fb, klag, klag)


def _longconv_kernel(u_ref, gate_ref, sk_ref, skip_ref, fu_ref, inv_ref, o_ref, spec_ref, y_ref, *, c, n_ch, order):
    for j in range(n_ch):
        spec_ref[j] = jnp.dot(fu_ref[...], u_ref[0, j * c:(j + 1) * c, :],
                              preferred_element_type=F32).astype(BF16)
    skip = skip_ref[order:order + 1, :]

    def product_slab(r, carry):
        r0 = pl.multiple_of(r * HY_SLAB, HY_SLAB)
        re_rows = pl.ds(r0, HY_SLAB)
        im_rows = pl.ds(c + r0, HY_SLAB)
        for i in range(n_ch):
            top = None
            bot = None
            for j in range(n_ch):
                dl = i - j + n_ch - 1
                xr = spec_ref[j, re_rows, :]
                xi = spec_ref[j, im_rows, :]
                gr = sk_ref[0, dl, re_rows, :]
                gi = sk_ref[0, dl, im_rows, :]
                t = xr * gr - xi * gi
                b = xr * gi + xi * gr
                top = t if top is None else top + t
                bot = b if bot is None else bot + b
            y_ref[i, re_rows, :] = top
            y_ref[i, im_rows, :] = bot
        return carry

    lax.fori_loop(0, c // HY_SLAB, product_slab, 0)
    for i in range(n_ch):
        y = jnp.dot(inv_ref[...], y_ref[i], preferred_element_type=F32)
        rows = slice(i * c, (i + 1) * c)
        u = u_ref[0, rows, :].astype(F32)
        o_ref[0, rows, :] = (gate_ref[0, rows, :].astype(F32) * (y + u * skip)).astype(o_ref.dtype)


def long_conv_gate(u_src, u_col0, gate_src, gate_col0, spectra, order, skip, fwd_u, inv, c, *, tc=512):
    b, n, _ = u_src.shape
    d = spectra.shape[3]
    n_ch = n // c
    n_delta = spectra.shape[1]
    nj = d // tc
    return pl.pallas_call(
        functools.partial(_longconv_kernel, c=c, n_ch=n_ch, order=order),
        out_shape=jax.ShapeDtypeStruct((b, n, d), BF16),
        grid=(nj, b),
        in_specs=[pl.BlockSpec((1, n, tc), lambda j, bi: (bi, 0, u_col0 * nj + j)),
                  pl.BlockSpec((1, n, tc), lambda j, bi: (bi, 0, gate_col0 * nj + j)),
                  pl.BlockSpec((1, n_delta, 2 * c, tc), lambda j, bi: (order, 0, 0, j)),
                  pl.BlockSpec((HY_ORDER, tc), lambda j, bi: (0, j)),
                  pl.BlockSpec((2 * c, c), lambda j, bi: (0, 0)),
                  pl.BlockSpec((c, 2 * c), lambda j, bi: (0, 0))],
        out_specs=pl.BlockSpec((1, n, tc), lambda j, bi: (bi, 0, j)),
        scratch_shapes=[pltpu.VMEM((n_ch, 2 * c, tc), BF16), pltpu.VMEM((n_ch, 2 * c, tc), BF16)],
        compiler_params=_cparams(("parallel", "parallel")),
        name="long_conv_gate",
    )(u_src, gate_src, spectra, skip.astype(F32), fwd_u, inv)


def _router_kernel(h_ref, w_ref, o_ref):
    logits = _dot3(h_ref[...], w_ref[...])
    lane = lax.broadcasted_iota(jnp.int32, logits.shape, 1).astype(F32)
    lg = jnp.where(lane < N_EXPERTS, logits, -jnp.inf)
    m1 = lg.max(axis=-1, keepdims=True)
    e1 = jnp.where(lg == m1, lane, float(LANES)).min(axis=-1, keepdims=True)
    lg2 = jnp.where(lane == e1, -jnp.inf, lg)
    m2 = lg2.max(axis=-1, keepdims=True)
    e2 = jnp.where(lg2 == m2, lane, float(LANES)).min(axis=-1, keepdims=True)
    t = jnp.exp(m2 - m1)
    p1 = 1.0 / (1.0 + t)
    p2 = t / (1.0 + t)
    o_ref[...] = jnp.where(lane == 0, e1, jnp.where(lane == 1, e2, jnp.where(lane == 2, p1,
                           jnp.where(lane == 3, p2, 0.0))))


def moe_route(h, router_w, *, tm=512):
    t, d = h.shape
    w = jnp.pad(router_w.astype(F32), ((0, 0), (0, LANES - N_EXPERTS)))
    return pl.pallas_call(
        _router_kernel,
        out_shape=jax.ShapeDtypeStruct((t, LANES), F32),
        grid=(t // tm,),
        in_specs=[pl.BlockSpec((tm, d), lambda i: (i, 0)), pl.BlockSpec((d, LANES), lambda i: (0, 0))],
        out_specs=pl.BlockSpec((tm, LANES), lambda i: (i, 0)),
        compiler_params=_cparams(("parallel",)),
        name="moe_router",
    )(h, w)


def _row_copy(src_ref, dst_ref, sem, src_row, dst_row):
    return pltpu.make_async_copy(src_ref.at[pl.ds(src_row, 1)], dst_ref.at[pl.ds(dst_row, 1)], sem)


def _load_indices(idx_vmem_ref, idx_smem, isem):
    cp = pltpu.make_async_copy(idx_vmem_ref, idx_smem, isem)
    cp.start()
    cp.wait()


def _issue_rows(idx_smem, idx_row, src_ref, buf_ref, sem, n):
    def issue(blk, carry):
        for u in range(GATHER_UNROLL):
            r = blk * GATHER_UNROLL + u
            _row_copy(src_ref, buf_ref, sem, idx_smem[idx_row, r], r).start()
        return carry

    lax.fori_loop(0, n // GATHER_UNROLL, issue, 0)


def _wait_rows(src_ref, buf_ref, sem, n):
    pltpu.make_async_copy(src_ref.at[pl.ds(0, n)], buf_ref, sem).wait()


def _gather_one_step_ahead(step, n_active, idx_refs, idx_next_refs, src_ref, idx_smem, buf_ref, sems, isem, tm):
    n_idx = len(idx_refs)
    slot = lax.rem(step, 2)

    def start(refs, s):
        for q, idx_ref in enumerate(refs):
            row = s * n_idx + q
            _load_indices(idx_ref.at[0], idx_smem.at[pl.ds(row, 1)], isem)
            _issue_rows(idx_smem, row, src_ref, buf_ref.at[s, q], sems.at[s, q], tm)

    @pl.when(step == 0)
    def _():
        start(idx_refs, 0)

    @pl.when(step + 1 < n_active)
    def _():
        start(idx_next_refs, 1 - slot)

    for q in range(n_idx):
        _wait_rows(src_ref, buf_ref.at[slot, q], sems.at[slot, q], tm)
    return slot


def _dispatch_kernel(nu_ref, idx_ref, idx_next_ref, src_ref, o_ref, idx_smem, buf_ref, sems, isem, *, tm):
    step = pl.program_id(0)
    used = step < nu_ref[0]

    @pl.when(used)
    def _():
        slot = _gather_one_step_ahead(step, nu_ref[0], [idx_ref], [idx_next_ref], src_ref, idx_smem, buf_ref,
                                      sems, isem, tm)
        o_ref[...] = buf_ref[slot, 0].astype(o_ref.dtype)

    @pl.when(jnp.logical_not(used))
    def _():
        o_ref[...] = jnp.zeros_like(o_ref)


def moe_dispatch(h, buf_tok, n_used, *, tm=MOE_TILE):
    _, d = h.shape
    n_rows = buf_tok.shape[0]
    n_blocks = n_rows // tm
    idx = buf_tok.reshape(n_blocks, 1, tm)
    return pl.pallas_call(
        functools.partial(_dispatch_kernel, tm=tm),
        out_shape=jax.ShapeDtypeStruct((n_rows, d), BF16),
        grid_spec=pltpu.PrefetchScalarGridSpec(
            num_scalar_prefetch=1,
            grid=(n_blocks,),
            in_specs=[pl.BlockSpec((1, 1, tm), lambda i, nu: (i, 0, 0)),
                      pl.BlockSpec((1, 1, tm), lambda i, nu: (jnp.minimum(i + 1, n_blocks - 1), 0, 0)),
                      pl.BlockSpec(memory_space=pl.ANY)],
            out_specs=pl.BlockSpec((tm, d), lambda i, nu: (i, 0)),
            scratch_shapes=[pltpu.SMEM((2, tm), jnp.int32), pltpu.VMEM((2, 1, tm, d), F32),
                            pltpu.SemaphoreType.DMA((2, 1)), pltpu.SemaphoreType.DMA],
        ),
        compiler_params=_cparams(("arbitrary",)),
        name="moe_dispatch",
    )(n_used, idx, idx, h)


def _combine_kernel(i0_ref, i1_ref, i0_next_ref, i1_next_ref, yb_ref, slab_ref, o_ref, idx_smem, buf_ref,
                    sems, isem, *, tm):
    slot = _gather_one_step_ahead(pl.program_id(0), pl.num_programs(0), [i0_ref, i1_ref],
                                  [i0_next_ref, i1_next_ref], yb_ref, idx_smem, buf_ref, sems, isem, tm)
    o_ref[...] = buf_ref[slot, 0] * slab_ref[:, 2:3] + buf_ref[slot, 1] * slab_ref[:, 3:4]


def moe_combine(yb, pos0, pos1, slab, *, tm=512):
    t = pos0.shape[0]
    d = yb.shape[1]
    n_blocks = t // tm
    idx_spec = pl.BlockSpec((1, 1, tm), lambda i: (i, 0, 0))
    idx_next_spec = pl.BlockSpec((1, 1, tm), lambda i: (jnp.minimum(i + 1, n_blocks - 1), 0, 0))
    idx0 = pos0.reshape(n_blocks, 1, tm)
    idx1 = pos1.reshape(n_blocks, 1, tm)
    return pl.pallas_call(
        functools.partial(_combine_kernel, tm=tm),
        out_shape=jax.ShapeDtypeStruct((t, d), F32),
        grid=(n_blocks,),
        in_specs=[idx_spec, idx_spec, idx_next_spec, idx_next_spec, pl.BlockSpec(memory_space=pl.ANY),
                  pl.BlockSpec((tm, LANES), lambda i: (i, 0))],
        out_specs=pl.BlockSpec((tm, d), lambda i: (i, 0)),
        scratch_shapes=[pltpu.SMEM((4, tm), jnp.int32), pltpu.VMEM((2, 2, tm, d), F32),
                        pltpu.SemaphoreType.DMA((2, 2)), pltpu.SemaphoreType.DMA],
        compiler_params=_cparams(("arbitrary",)),
        name="moe_combine",
    )(idx0, idx1, idx0, idx1, yb, slab)


def moe_routing_tables(slab, tile):
    t = slab.shape[0]
    flat_e = slab[:, :2].astype(jnp.int32).reshape(-1)
    n_assign = flat_e.shape[0]
    onehot = (flat_e[:, None] == jnp.arange(N_EXPERTS)[None, :]).astype(jnp.int32)
    csum = jnp.cumsum(onehot, axis=0)
    counts = csum[-1]
    rank = jnp.take_along_axis(csum, flat_e[:, None], axis=1)[:, 0] - 1
    padded = (counts + tile - 1) // tile * tile
    pad_end = jnp.cumsum(padded)
    pad_start = pad_end - padded
    dest = (pad_start[flat_e] + rank).astype(jnp.int32)
    n_blocks = (n_assign + N_EXPERTS * (tile - 1) + tile - 1) // tile
    buf_tok = jnp.zeros((n_blocks * tile,), jnp.int32).at[dest].set(
        jnp.arange(n_assign, dtype=jnp.int32) // 2, unique_indices=True)
    block_e = jnp.minimum(jnp.searchsorted(pad_end, jnp.arange(n_blocks) * tile, side='right'),
                          N_EXPERTS - 1).astype(jnp.int32)
    dest = dest.reshape(t, 2)
    n_used = (pad_end[-1:] // tile).astype(jnp.int32)
    return buf_tok, block_e, n_used, dest[:, 0], dest[:, 1]


def moe_swiglu(h, router_w, w1, w3, w2, layer):
    slab = moe_route(h, router_w)
    buf_tok, block_e, n_used, pos0, pos1 = moe_routing_tables(slab, MOE_TILE)
    block_e = block_e + layer * N_EXPERTS
    xb = moe_dispatch(h, buf_tok, n_used)
    mid = swiglu_up(xb, w1, w3, block_e, tm=MOE_TILE, tn=512, n_used=n_used)
    yb = gmm(mid, w2, block_e, tm=MOE_TILE, tn=512, out_dtype=F32, n_used=n_used)
    return moe_combine(yb, pos0, pos1, slab)


def _row_tile(k):
    return 1024 if k <= 2048 else 512


def _dense_ids(n_rows, tm, idx):
    return jnp.full((n_rows // tm,), idx, jnp.int32)


def _col_tile(k, tm, n_out, first_col, out_bytes):
    budget = VMEM_LIMIT - 12 * 1024 * 1024
    for tn in (1024, 768, 512, 256):
        if n_out % tn or first_col % tn:
            continue
        working = k * tn * (4 + 2) + 2 * tm * k * 2 + 2 * tm * tn * out_bytes + tm * tn * 4
        if working <= budget:
            return tn
    raise ValueError("no column tile fits")


def _dense(a3, w, idx, *, bias=None, out_dtype=BF16, first_col=0, n_out=None):
    parts = tuple(a3) if isinstance(a3, (tuple, list)) else (a3,)
    b, s, _ = parts[0].shape
    k = w.shape[1]
    m = b * s
    tm = min(_row_tile(k), m)
    n_out = w.shape[2] if n_out is None else n_out
    tn = _col_tile(k, tm, n_out, first_col, jnp.dtype(out_dtype).itemsize)
    out = gmm(tuple(p.reshape(m, p.shape[2]) for p in parts), w, _dense_ids(m, tm, idx), bias, tm=tm, tn=tn,
              out_dtype=out_dtype, col_off=first_col // tn, n_out=n_out)
    return out.reshape(b, s, -1)


def _dense_swiglu(h3, w1, w3, w2, idx):
    b, s, k = h3.shape
    m = b * s
    tm = min(1024, m)
    mid = swiglu_up(h3.reshape(m, k), w1, w3, _dense_ids(m, tm, idx), tm=tm, tn=512)
    tm2 = min(_row_tile(mid.shape[1]), m)
    return gmm(mid, w2, _dense_ids(m, tm2, idx), tm=tm2, tn=512, out_dtype=BF16).reshape(b, s, -1)


def _attention_layer(h, hc, i, w_in, w_out, q_g, k_g, bias, cos2, sin2, ctx_out):
    d = h.shape[-1]
    n_na = d // (2 * HEAD_DIM)
    n_q = n_na
    n_kv = max(1, n_q // 4)
    group = n_q // n_kv
    c_bq, c_ak, c_av = n_na, n_na + n_q, 2 * n_na + n_q
    c_bk, c_bv = 3 * n_na + n_q, 3 * n_na + n_q + n_kv
    q_cols = (n_na + n_q) * HEAD_DIM
    proj = _dense(h, w_in, i)
    if ctx_out:
        projc = _dense(hc, w_in, i)
        shift = 0
    else:
        projc = _dense(hc, w_in, i, first_col=q_cols, n_out=w_in.shape[2] - q_cols)
        shift = n_na + n_q
    qb = head_prep(proj, c_bq, n_q, q_g, cos2, sin2, rope=True)
    kb = head_prep(proj, c_bk, n_kv, k_g, cos2, sin2, rope=True)
    kbc = head_prep(projc, c_bk - shift, n_kv, k_g, cos2, sin2, rope=False)
    o_a = na_attention(proj, projc, n_na, 0, c_ak, c_av, c_ak - shift, c_av - shift, bias)
    o_b = full_attention(qb, 0, n_kv, group, [(kbc, 0), (kb, 0)], [(projc, c_bv - shift), (proj, c_bv)])
    y = _dense((o_a, o_b), w_out, i)
    if not ctx_out:
        return y, None
    qbc = head_prep(projc, c_bq, n_q, q_g, cos2, sin2, rope=False)
    oc_a = full_attention(projc, 0, n_na, 1, [(projc, c_ak)], [(projc, c_av)])
    oc_b = full_attention(qbc, 0, n_kv, group, [(kbc, 0)], [(projc, c_bv)])
    yc = _dense((oc_a, oc_b), w_out, i)
    return y, yc


def _hyena_layer(h, i, p):
    (w_in, b_in, conv_w, conv_b, f_w1, f_b1, f_w2, f_b2, f_w3, freq, skip, w_out, b_out) = p
    n_tok = h.shape[1]
    c = min(HY_CHUNK, n_tok)
    u0 = _dense(h, w_in, i, bias=b_in[i].reshape(1, -1))
    u = short_conv(u0, conv_w[i], conv_b[i])
    fwd_u, fwd_g, inv = dft_matrices(c)
    klag = hyena_lag_filters(n_tok, f_w1[i], f_b1[i], f_w2[i], f_b2[i], f_w3[i], freq[i])
    spectra = filter_spectra(klag, fwd_g, c)
    z = long_conv_gate(u, 0, u, 1, spectra, 0, skip[i], fwd_u, inv, c)
    z = long_conv_gate(z, 0, u, 2, spectra, 1, skip[i], fwd_u, inv, c)
    return _dense(z, w_out, i, bias=b_out[i].reshape(1, -1))


def kernel(x, c, ctx, c_ctx, mod_w, mod_b, norm_g, att_w_in, att_w_out, att_q_norm_g, att_k_norm_g, na_rpb,
           ffn_w1, ffn_w3, ffn_w2, hy_w_in, hy_b_in, hy_conv_w, hy_conv_b, hy_f_w1, hy_f_b1, hy_f_w2, hy_f_b2,
           hy_f_w3, hy_freq, hy_skip, hy_w_out, hy_b_out, moe_router, moe_w1, moe_w3, moe_w2):
    depth = mod_w.shape[0]
    b, s, d = x.shape
    n_ctx = ctx.shape[1]
    d_ff = moe_w1.shape[-1]
    cos2, sin2 = rope_tables(s)
    moe_w1f = moe_w1.reshape(-1, d, d_ff)
    moe_w3f = moe_w3.reshape(-1, d, d_ff)
    moe_w2f = moe_w2.reshape(-1, d_ff, d)
    hy = (hy_w_in, hy_b_in, hy_conv_w, hy_conv_b, hy_f_w1, hy_f_b1, hy_f_w2, hy_f_b2, hy_f_w3, hy_freq,
          hy_skip, hy_w_out, hy_b_out)

    n_mod_rows = 16
    cond = jnp.concatenate([c, c_ctx[None, :], jnp.zeros((n_mod_rows - b - 1, d), F32)], axis=0)

    lat_mods, ctx_mods = [], []
    for l in range(depth):
        mod = gmm(cond, mod_w, _dense_ids(n_mod_rows, n_mod_rows, l), mod_b[l].reshape(1, -1),
                  tm=n_mod_rows, tn=1024, out_dtype=F32, silu_in=True)
        m_lat = mod[:b].reshape(b, 6, 1, d)
        lat_mods.append([m_lat[:, k] for k in range(6)])
        m_ctx = jnp.broadcast_to(mod[b].reshape(1, 6, 1, d), (b, 6, 1, d))
        ctx_mods.append([m_ctx[:, k] for k in range(6)])

    xc = ctx
    h = modulate(x, norm_g[0, 0], lat_mods[0][0], lat_mods[0][1], out_dtype=BF16)
    for l in range(depth):
        even = l % 2 == 0
        i = l // 2
        ctx_out = any(m % 2 == 0 for m in range(l + 1, depth))
        ctx_in = even or ctx_out
        sh1, sc1, g1, sh2, sc2, g2 = lat_mods[l]
        csh1, csc1, cg1, csh2, csc2, cg2 = ctx_mods[l]

        hc = modulate(xc, norm_g[l, 0], csh1, csc1, out_dtype=BF16) if ctx_in else None
        if even:
            bias = na_bias_blocks(na_rpb[i])
            y, yc = _attention_layer(h, hc, i, att_w_in, att_w_out, att_q_norm_g[i], att_k_norm_g[i], bias,
                                     cos2, sin2, ctx_out)
        else:
            y = _hyena_layer(h, i, hy)
            yc = _hyena_layer(hc, i, hy) if ctx_out else None
        ffn_in = BF16 if even else F32
        x, h = residual_modulate(x, y, norm_g[l, 1], g1, norm_g[l, 2], sh2, sc2, out_dtype=ffn_in)
        if ctx_out:
            xc = residual(xc, yc, norm_g[l, 1], cg1)
            hc = modulate(xc, norm_g[l, 2], csh2, csc2, out_dtype=ffn_in)
        y_off = yc_off = 0
        if even:
            y = _dense_swiglu(h, ffn_w1, ffn_w3, ffn_w2, i)
            yc = _dense_swiglu(hc, ffn_w1, ffn_w3, ffn_w2, i) if ctx_out else None
        elif ctx_out and (b * n_ctx) % s == 0:
            tokens = jnp.concatenate([h.reshape(b * s, d), hc.reshape(b * n_ctx, d)], axis=0)
            y_all = moe_swiglu(tokens, moe_router[i], moe_w1f, moe_w3f, moe_w2f, i)
            y = y_all.reshape(-1, s, d)
            yc = y_all.reshape(-1, n_ctx, d)
            yc_off = b * s // n_ctx
        else:
            y = moe_swiglu(h.reshape(b * s, d), moe_router[i], moe_w1f, moe_w3f, moe_w2f, i).reshape(b, s, d)
            yc = None
            if ctx_out:
                yc = moe_swiglu(hc.reshape(b * n_ctx, d), moe_router[i], moe_w1f, moe_w3f, moe_w2f,
                                i).reshape(b, n_ctx, d)
        if l + 1 < depth:
            x, h = residual_modulate(x, y, norm_g[l, 3], g2, norm_g[l + 1, 0], lat_mods[l + 1][0],
                                     lat_mods[l + 1][1], out_dtype=BF16, y_off=y_off)
        else:
            x = residual(x, y, norm_g[l, 3], g2, y_off=y_off)
        if ctx_out:
            xc = residual(xc, yc, norm_g[l, 3], cg2, y_off=yc_off)
    return x
```
